```python
import jax, jax.numpy as jnp
from jax import lax
import numpy as np

D_MODEL = 1024
BATCH = 1
SEQ = 16384
DEPTH = 4
DEC_BATCH = 8
DEC_SEQ = 64
PAST_LEN = 2048

CHUNK = 64
Q_BLOCK = 128
K_BLOCK = 128
N_SWEEP_GROUPS = 16
H_MLA = 4
D_NOPE = 128
D_ROPE = 64
D_VA = 128
D_LATENT = 512
H_SB = 4
D_SB = 128
D_FF = 2816
CONV_W = 3
ROPE_THETA = 10000.0
EPS = 1e-6
QA_COLS = H_MLA * (D_NOPE + D_ROPE)
SB_COLS = H_SB * D_SB
IN_COLS = QA_COLS + D_LATENT + D_ROPE + 3 * SB_COLS
MIX_WIDTH = H_MLA * D_VA + H_SB * D_SB

kernel_name = "hymba_mla_stickbreak_convffn_stream_step"


def _rms(x, g):
    xf = x.astype(jnp.float32)
    y = xf * lax.rsqrt(jnp.mean(xf * xf, axis=-1, keepdims=True) + EPS)
    return (y * g.astype(jnp.float32)).astype(x.dtype)


def _rope(x, pos):
    half = D_ROPE // 2
    freqs = ROPE_THETA ** (-jnp.arange(half, dtype=jnp.float32) / half)
    ang = pos.astype(jnp.float32)[:, None] * freqs[None, :]
    shape = (1, pos.shape[0]) + (1,) * (x.ndim - 3) + (half,)
    cos = jnp.cos(ang).reshape(shape)
    sin = jnp.sin(ang).reshape(shape)
    xf = x.astype(jnp.float32)
    x1, x2 = xf[..., :half], xf[..., half:]
    return jnp.concatenate([x1 * cos - x2 * sin, x2 * cos + x1 * sin], axis=-1).astype(x.dtype)


def _sweep(attend, qs, qpos, ks, kpos):
    T = qpos.shape[0]
    S = kpos.shape[0]
    P = S - T
    if T <= Q_BLOCK or T % Q_BLOCK:
        return attend(qs, qpos, ks, kpos)
    nb = T // Q_BLOCK
    groups = min(N_SWEEP_GROUPS, nb)
    while nb % groups:
        groups -= 1
    per = nb // groups
    outs = []
    for g in range(groups):
        q0, q1 = g * per * Q_BLOCK, (g + 1) * per * Q_BLOCK
        kend = P + q1
        ks_g = tuple(k[:, :kend] for k in ks)
        kpos_g = kpos[:kend]

        def split(a, q0=q0, q1=q1):
            return jnp.moveaxis(a[:, q0:q1].reshape(a.shape[0], per, Q_BLOCK, *a.shape[2:]), 1, 0)

        blocks = (tuple(split(q) for q in qs), qpos[q0:q1].reshape(per, Q_BLOCK))
        out = lax.map(lambda a, ks_g=ks_g, kpos_g=kpos_g: attend(a[0], a[1], ks_g, kpos_g), blocks)
        outs.append(jnp.moveaxis(out, 0, 1).reshape(out.shape[1], per * Q_BLOCK, *out.shape[3:]))
    return jnp.concatenate(outs, axis=1)


def _rev_excl_cumsum(x):
    S = x.shape[-1]
    nk = -(-S // K_BLOCK)
    xp = jnp.pad(x, [(0, 0)] * (x.ndim - 1) + [(0, nk * K_BLOCK - S)])
    xr = xp.reshape(*x.shape[:-1], nk, K_BLOCK)
    ib = jnp.arange(K_BLOCK)
    tri = (ib[:, None] > ib[None, :]).astype(x.dtype)
    within = jnp.einsum('...nj,ji->...ni', xr, tri)
    nbk = jnp.arange(nk)
    tri_b = (nbk[:, None] > nbk[None, :]).astype(x.dtype)
    later = jnp.einsum('...m,mn->...n', xr.sum(-1), tri_b)
    return (within + later[..., None]).reshape(*x.shape[:-1], nk * K_BLOCK)[..., :S]


def _mla_attend(qs, qpos, ks, kpos):
    q_nope, q_pe = qs
    k_nope, k_pe, v = ks
    s = (jnp.einsum('bqhd,bkhd->bhqk', q_nope, k_nope)
         + jnp.einsum('bqhr,bkr->bhqk', q_pe, k_pe)).astype(jnp.float32) * (D_NOPE + D_ROPE) ** -0.5
    visible = (kpos[None, :] // CHUNK) <= (qpos[:, None] // CHUNK)
    p = jax.nn.softmax(jnp.where(visible, s, -1e30), axis=-1)
    return jnp.einsum('bhqk,bkhd->bqhd', p.astype(v.dtype), v)


def _sb_attend(qs, qpos, ks, kpos):
    (q,) = qs
    k, v = ks
    z = jnp.einsum('bqhd,bkhd->bhqk', q, k).astype(jnp.float32) * D_SB ** -0.5
    before = kpos[None, :] < qpos[:, None]
    ls_neg = jax.nn.log_sigmoid(-z)
    log_rest = _rev_excl_cumsum(jnp.where(before, ls_neg, 0.0))
    w = jnp.where(before, jnp.exp(z + ls_neg + log_rest), 0.0)
    return jnp.einsum('bhqk,bkhd->bqhd', w.astype(v.dtype), v)


def _layer(x, c, past_lat, past_kpe, past_k, past_v, conv_buf,
           w_ada, b_ada, g_norm_mix, g_norm_ffn, w_in, g_kv_latent, g_q_nope, g_q_rope,
           g_k_nope, g_k_rope, w_uk, w_uv, g_out_mla, g_out_sb, w_out,
           w_gate, w_up, w_conv, b_conv, w_down):
    B, T, _ = x.shape
    P = past_lat.shape[1]
    S = P + T
    qpos = P + jnp.arange(T, dtype=jnp.int32)
    kpos = jnp.arange(S, dtype=jnp.int32)

    mod = jax.nn.silu(c) @ w_ada + b_ada
    sh_m, sc_m, gt_m, sh_f, sc_f, gt_f = [m[:, None, :] for m in jnp.split(mod, 6, axis=-1)]

    h = _rms(x, g_norm_mix) * (1 + sc_m) + sh_m
    proj = h @ w_in
    o1 = QA_COLS
    o2 = o1 + D_LATENT
    o3 = o2 + D_ROPE
    o4 = o3 + SB_COLS
    o5 = o4 + SB_COLS
    q_a, lat, kpe, q_b, k_b, v_b = jnp.split(proj, [o1, o2, o3, o4, o5], axis=-1)

    q_a = q_a.reshape(B, T, H_MLA, D_NOPE + D_ROPE)
    q_nope = _rms(q_a[..., :D_NOPE], g_q_nope)
    q_pe = _rope(_rms(q_a[..., D_NOPE:], g_q_rope), qpos)
    lat = _rms(lat, g_kv_latent)
    kpe = _rope(_rms(kpe, g_k_rope), qpos)
    lat_all = jnp.concatenate([past_lat, lat], axis=1)
    kpe_all = jnp.concatenate([past_kpe, kpe], axis=1)
    k_nope = _rms((lat_all @ w_uk).reshape(B, S, H_MLA, D_NOPE), g_k_nope)
    v_a = (lat_all @ w_uv).reshape(B, S, H_MLA, D_VA)
    o_a = _sweep(_mla_attend, (q_nope, q_pe), qpos, (k_nope, kpe_all, v_a), kpos)

    q_b = q_b.reshape(B, T, H_SB, D_SB)
    k_b = k_b.reshape(B, T, H_SB, D_SB)
    v_b = v_b.reshape(B, T, H_SB, D_SB)
    k_all = jnp.concatenate([past_k, k_b], axis=1)
    v_all = jnp.concatenate([past_v, v_b], axis=1)
    o_b = _sweep(_sb_attend, (q_b,), qpos, (k_all, v_all), kpos)

    merged = jnp.concatenate([_rms(o_a.reshape(B, T, H_MLA * D_VA), g_out_mla),
                              _rms(o_b.reshape(B, T, H_SB * D_SB), g_out_sb)], axis=-1)
    x = x + gt_m * (merged @ w_out)

    h = _rms(x, g_norm_ffn) * (1 + sc_f) + sh_f
    u = h @ w_gate
    u_ext = jnp.concatenate([conv_buf, u], axis=1)
    u_conv = b_conv + sum(u_ext[:, i:i + T] * w_conv[i] for i in range(CONV_W))
    x = x + gt_f * ((jax.nn.silu(u_conv) * (h @ w_up)) @ w_down)
    return x, lat, kpe, k_b, v_b, u_ext[:, T:]


def setup_inputs(seed: int = 0) -> dict:
    key = jax.random.key(seed)
    ks = iter(jax.random.split(key, 40))

    def nrm(shape, scale=1.0):
        return jax.random.normal(next(ks), shape, jnp.float32) * scale

    def gain(shape):
        return 1.0 + nrm(shape, 0.05)

    return {
        'x_prompt': nrm((BATCH, SEQ, D_MODEL)),
        'x_sample': nrm((DEC_BATCH, DEC_SEQ, D_MODEL)),
        'c_prompt': nrm((BATCH, D_MODEL)),
        'c_sample': nrm((DEC_BATCH, D_MODEL)),
        'cache_mla_latent': nrm((DEPTH, DEC_BATCH, PAST_LEN, D_LATENT)),
        'cache_mla_krope': nrm((DEPTH, DEC_BATCH, PAST_LEN, D_ROPE)),
        'cache_sb_k': nrm((DEPTH, DEC_BATCH, PAST_LEN, H_SB, D_SB)),
        'cache_sb_v': nrm((DEPTH, DEC_BATCH, PAST_LEN, H_SB, D_SB)),
        'state_ffn_conv': nrm((DEPTH, DEC_BATCH, CONV_W - 1, D_FF)),
        'w_ada': nrm((DEPTH, D_MODEL, 6 * D_MODEL), 0.5 * D_MODEL ** -0.5),
        'b_ada': nrm((DEPTH, 6 * D_MODEL), 0.02),
        'g_norm_mix': gain((DEPTH, D_MODEL)),
        'g_norm_ffn': gain((DEPTH, D_MODEL)),
        'w_in': nrm((DEPTH, D_MODEL, IN_COLS), D_MODEL ** -0.5),
        'g_kv_latent': gain((DEPTH, D_LATENT)),
        'g_q_nope': gain((DEPTH, D_NOPE)),
        'g_q_rope': gain((DEPTH, D_ROPE)),
        'g_k_nope': gain((DEPTH, D_NOPE)),
        'g_k_rope': gain((DEPTH, D_ROPE)),
        'w_uk': nrm((DEPTH, D_LATENT, H_MLA * D_NOPE), D_LATENT ** -0.5),
        'w_uv': nrm((DEPTH, D_LATENT, H_MLA * D_VA), D_LATENT ** -0.5),
        'g_out_mla': gain((DEPTH, H_MLA * D_VA)),
        'g_out_sb': gain((DEPTH, H_SB * D_SB)),
        'w_out': nrm((DEPTH, MIX_WIDTH, D_MODEL), MIX_WIDTH ** -0.5),
        'w_gate': nrm((DEPTH, D_MODEL, D_FF), D_MODEL ** -0.5),
        'w_up': nrm((DEPTH, D_MODEL, D_FF), D_MODEL ** -0.5),
        'w_conv': nrm((DEPTH, CONV_W, D_FF), CONV_W ** -0.5),
        'b_conv': nrm((DEPTH, D_FF), 0.01),
        'w_down': nrm((DEPTH, D_FF, D_MODEL), D_FF ** -0.5),
    }


def reference(x_prompt, x_sample, c_prompt, c_sample, cache_mla_latent, cache_mla_krope,
              cache_sb_k, cache_sb_v, state_ffn_conv, w_ada, b_ada, g_norm_mix, g_norm_ffn,
              w_in, g_kv_latent, g_q_nope, g_q_rope, g_k_nope, g_k_rope, w_uk, w_uv,
              g_out_mla, g_out_sb, w_out, w_gate, w_up, w_conv, b_conv, w_down):
    dt = x_prompt.dtype
    bp = x_prompt.shape[0]
    empty_lat = jnp.zeros((bp, 0, D_LATENT), dt)
    empty_kpe = jnp.zeros((bp, 0, D_ROPE), dt)
    empty_kv = jnp.zeros((bp, 0, H_SB, D_SB), dt)
    zero_buf = jnp.zeros((bp, CONV_W - 1, D_FF), dt)

    xp, xs = x_prompt, x_sample
    lat_p, kpe_p, k_p, v_p, conv_p = [], [], [], [], []
    lat_s, kpe_s, k_s, v_s, conv_s = [], [], [], [], []
    for l in range(DEPTH):
        w = (w_ada[l], b_ada[l], g_norm_mix[l], g_norm_ffn[l], w_in[l], g_kv_latent[l],
             g_q_nope[l], g_q_rope[l], g_k_nope[l], g_k_rope[l], w_uk[l], w_uv[l],
             g_out_mla[l], g_out_sb[l], w_out[l], w_gate[l], w_up[l], w_conv[l], b_conv[l], w_down[l])
        xp, a1, a2, a3, a4, a5 = _layer(xp, c_prompt, empty_lat, empty_kpe, empty_kv, empty_kv,
                                        zero_buf, *w)
        lat_p.append(a1); kpe_p.append(a2); k_p.append(a3); v_p.append(a4); conv_p.append(a5)
        xs, b1, b2, b3, b4, b5 = _layer(xs, c_sample, cache_mla_latent[l], cache_mla_krope[l],
                                        cache_sb_k[l], cache_sb_v[l], state_ffn_conv[l], *w)
        lat_s.append(b1); kpe_s.append(b2); k_s.append(b3); v_s.append(b4); conv_s.append(b5)

    return (xp, xs,
            jnp.stack(lat_p), jnp.stack(kpe_p), jnp.stack(k_p), jnp.stack(v_p), jnp.stack(conv_p),
            jnp.stack(lat_s), jnp.stack(kpe_s), jnp.stack(k_s), jnp.stack(v_s), jnp.stack(conv_s))
```

```python
import functools

import jax
import jax.numpy as jnp
from jax import lax
from jax.experimental import pallas as pl
from jax.experimental.pallas import tpu as pltpu

F32 = jnp.float32
BF16 = jnp.bfloat16

H_MLA = 4
D_NOPE = 128
D_ROPE = 64
D_VA = 128
D_LATENT = 512
H_SB = 4
D_SB = 128
CONV_W = 3
CHUNK = 64
ROPE_THETA = 10000.0
EPS = 1e-6

LANES = 128
Q_HEAD_COLS = 2 * LANES
MASK_VALUE = -1e30
SB_EXP_ZERO = -105.0
VMEM_LIMIT_BYTES = 56 * 1024 * 1024

MLA_SCALE = (D_NOPE + D_ROPE) ** -0.5
SB_SCALE = D_SB ** -0.5


def _rms_rows(x, g):
    return x * lax.rsqrt(jnp.mean(x * x, axis=-1, keepdims=True) + EPS) * g


def _dot(a, b):
    return jnp.dot(a, b, preferred_element_type=F32)


def _dot_nt(a, b):
    return lax.dot_general(a, b, (((1,), (1,)), ((), ())), preferred_element_type=F32)


def _dot_tn(a, b):
    return lax.dot_general(a, b, (((0,), (0,)), ((), ())), preferred_element_type=F32)


def _params(*sem):
    return pltpu.CompilerParams(dimension_semantics=sem, vmem_limit_bytes=VMEM_LIMIT_BYTES)


def _ada_kernel(c_ref, w_ref, b_ref, o_ref):
    c = c_ref[...]
    s = (c * jax.nn.sigmoid(c)).astype(BF16)
    o_ref[0] = _dot(s, w_ref[0].astype(BF16)) + b_ref[0]


def _ada_all(c_all, w_ada, b_ada):
    depth, d, n = w_ada.shape
    bp = c_all.shape[0]
    tn = d
    return pl.pallas_call(
        _ada_kernel,
        grid=(depth, n // tn),
        in_specs=[
            pl.BlockSpec((bp, d), lambda l, j: (0, 0)),
            pl.BlockSpec((1, d, tn), lambda l, j: (l, 0, j)),
            pl.BlockSpec((1, 1, tn), lambda l, j: (l, 0, j)),
        ],
        out_specs=pl.BlockSpec((1, bp, tn), lambda l, j: (l, 0, j)),
        out_shape=jax.ShapeDtypeStruct((depth, bp, n), F32),
        compiler_params=_params("parallel", "parallel"),
        name="ada_mod",
    )(c_all, w_ada, b_ada.reshape(depth, 1, n))


C_QM = 0
C_LAT = H_MLA * Q_HEAD_COLS
C_KPE = C_LAT + D_LATENT
C_QB = C_KPE + LANES
C_KB = C_QB + H_SB * D_SB
C_VB = C_KB + H_SB * D_SB
C_END = C_VB + H_SB * D_SB


def _rope_pair(v, g, cos, sin):
    ssq = jnp.sum(v * v, axis=-1, keepdims=True) * (0.5 / D_ROPE)
    vn = v * lax.rsqrt(ssq + EPS) * g
    return vn * cos + pltpu.roll(vn, D_ROPE, 1) * sin


def _inproj_kernel(x_ref, mod_ref, gn_ref, w_ref, gqn_ref, gqr_ref, glat_ref, gkr_ref, cos_ref, sin_ref,
                   qm_ref, lat_ref, kpe_ref, kpe128_ref, qb_ref, kb_ref, vb_ref, kb16_ref, vb16_ref,
                   h_scr, *, seg, nseg, d):
    for s in range(nseg):
        rows = slice(s * seg, (s + 1) * seg)
        m = mod_ref[s]
        h = _rms_rows(x_ref[rows, :], gn_ref[...]) * (1.0 + m[:, d:2 * d]) + m[:, 0:d]
        h_scr[rows, :] = h.astype(BF16)
    h = h_scr[...]
    cos = cos_ref[...]
    sin = sin_ref[...]

    for hd in range(H_MLA):
        c0 = C_QM + hd * Q_HEAD_COLS
        qh = _dot(h, w_ref[:, c0:c0 + Q_HEAD_COLS])
        qn = _rms_rows(qh[:, :D_NOPE], gqn_ref[...]) * MLA_SCALE
        qp = _rope_pair(qh[:, D_NOPE:], gqr_ref[...], cos, sin) * MLA_SCALE
        qm_ref[:, c0:c0 + D_NOPE] = qn.astype(BF16)
        qm_ref[:, c0 + D_NOPE:c0 + Q_HEAD_COLS] = qp.astype(BF16)

    lat_ref[...] = _rms_rows(_dot(h, w_ref[:, C_LAT:C_KPE]), glat_ref[...])

    kp = _rope_pair(_dot(h, w_ref[:, C_KPE:C_QB]), gkr_ref[...], cos, sin)
    kpe_ref[...] = kp[:, :D_ROPE]
    lane = lax.broadcasted_iota(jnp.int32, kp.shape, 1)
    kpe128_ref[...] = jnp.where(lane < D_ROPE, kp, 0.0).astype(BF16)

    qb_ref[...] = (_dot(h, w_ref[:, C_QB:C_KB]) * SB_SCALE).astype(BF16)
    kb = _dot(h, w_ref[:, C_KB:C_VB])
    kb_ref[...] = kb
    kb16_ref[...] = kb.astype(BF16)
    vb = _dot(h, w_ref[:, C_VB:C_END])
    vb_ref[...] = vb
    vb16_ref[...] = vb.astype(BF16)


def _row_tiling(batch, t, tm_max):
    if t >= tm_max:
        assert t % tm_max == 0
        return tm_max, tm_max, 1
    nseg = max(1, min(batch, tm_max // t))
    while batch % nseg:
        nseg -= 1
    return nseg * t, t, nseg


def _mod_spec(t, tm, nseg, width):
    if nseg == 1:
        per = t // tm
        return pl.BlockSpec((1, 1, width), lambda i: (i // per, 0, 0))
    return pl.BlockSpec((nseg, 1, width), lambda i: (i, 0, 0))


def _inproj(x2, mod3, gn, w_p, gqn, gqr, glat, gkr, cos, sin, *, batch, t, tm_max):
    r, d = x2.shape
    tm, seg, nseg = _row_tiling(batch, t, tm_max)
    full = lambda shape: pl.BlockSpec(shape, lambda i: (0,) * len(shape))
    rows = lambda w: pl.BlockSpec((tm, w), lambda i: (i, 0))
    sb = H_SB * D_SB
    out_shapes = (
        jax.ShapeDtypeStruct((r, H_MLA * Q_HEAD_COLS), BF16),
        jax.ShapeDtypeStruct((r, D_LATENT), F32),
        jax.ShapeDtypeStruct((r, D_ROPE), F32),
        jax.ShapeDtypeStruct((r, LANES), BF16),
        jax.ShapeDtypeStruct((r, sb), BF16),
        jax.ShapeDtypeStruct((r, sb), F32),
        jax.ShapeDtypeStruct((r, sb), F32),
        jax.ShapeDtypeStruct((r, sb), BF16),
        jax.ShapeDtypeStruct((r, sb), BF16),
    )
    return pl.pallas_call(
        functools.partial(_inproj_kernel, seg=seg, nseg=nseg, d=d),
        grid=(r // tm,),
        in_specs=[rows(d), _mod_spec(t, tm, nseg, 6 * d), full((1, d)), full(w_p.shape),
                  full((1, D_NOPE)), full((1, LANES)), full((1, D_LATENT)), full((1, LANES)),
                  rows(LANES), rows(LANES)],
        out_specs=[rows(s.shape[1]) for s in out_shapes],
        out_shape=out_shapes,
        scratch_shapes=[pltpu.VMEM((tm, d), BF16)],
        compiler_params=_params("parallel"),
        name="in_proj",
    )(x2, mod3, gn, w_p, gqn, gqr, glat, gkr, cos, sin)


def _kvup_kernel(lat_ref, kpe128_ref, wuk_ref, wuv_ref, gkn_ref, k_ref, v_ref):
    l16 = lat_ref[...].astype(BF16)
    kn = _dot(l16, wuk_ref[...])
    kpe = kpe128_ref[...]
    for hd in range(H_MLA):
        c0 = hd * Q_HEAD_COLS
        k = _rms_rows(kn[:, hd * D_NOPE:(hd + 1) * D_NOPE], gkn_ref[...])
        k_ref[:, c0:c0 + D_NOPE] = k.astype(BF16)
        k_ref[:, c0 + D_NOPE:c0 + Q_HEAD_COLS] = kpe
    v_ref[...] = _dot(l16, wuv_ref[...]).astype(BF16)


def _kvup(lat_all, kpe128_all, wuk, wuv, gkn, *, tm_max):
    r = lat_all.shape[0]
    tm = min(tm_max, r)
    while r % tm:
        tm //= 2
    full = lambda shape: pl.BlockSpec(shape, lambda i: (0,) * len(shape))
    rows = lambda w: pl.BlockSpec((tm, w), lambda i: (i, 0))
    return pl.pallas_call(
        _kvup_kernel,
        grid=(r // tm,),
        in_specs=[rows(D_LATENT), rows(LANES), full(wuk.shape), full(wuv.shape), full((1, D_NOPE))],
        out_specs=[rows(H_MLA * Q_HEAD_COLS), rows(H_MLA * D_VA)],
        out_shape=(jax.ShapeDtypeStruct((r, H_MLA * Q_HEAD_COLS), BF16),
                   jax.ShapeDtypeStruct((r, H_MLA * D_VA), BF16)),
        compiler_params=_params("parallel"),
        name="kv_up",
    )(lat_all, kpe128_all, wuk, wuv, gkn)


def _mla_kernel(q_ref, k_ref, v_ref, o_ref, m_scr, l_scr, acc_scr, *, tq, tk, past):
    i = pl.program_id(2)
    q = q_ref[...]
    m_scr[...] = jnp.full(m_scr.shape, MASK_VALUE, F32)
    l_scr[...] = jnp.zeros(l_scr.shape, F32)
    acc_scr[...] = jnp.zeros(acc_scr.shape, F32)

    def update(kb, vb, mask):
        s = _dot_nt(kb, q)
        if mask is not None:
            s = jnp.where(mask, s, MASK_VALUE)
        m_old = m_scr[...]
        m_new = jnp.maximum(m_old, jnp.max(s, axis=0, keepdims=True))
        alpha = jnp.exp(m_old - m_new)
        p = jnp.exp(s - m_new)
        l_scr[...] = alpha * l_scr[...] + jnp.sum(p, axis=0, keepdims=True)
        acc_scr[...] = alpha * acc_scr[...] + _dot_tn(vb, p.astype(BF16))
        m_scr[...] = m_new

    q0 = past + i * tq
    n_prev = q0 // tk

    def body(j, carry):
        r0 = pl.multiple_of(j * tk, tk)
        update(k_ref[pl.ds(r0, tk), :], v_ref[pl.ds(r0, tk), :], None)
        return carry

    lax.fori_loop(0, n_prev, body, 0)

    if tq > CHUNK:
        kc = lax.broadcasted_iota(jnp.int32, (tq, tq), 0) // CHUNK
        qc = lax.broadcasted_iota(jnp.int32, (tq, tq), 1) // CHUNK
        mask = kc <= qc
    else:
        mask = None
    d0 = pl.multiple_of(q0, tq)
    update(k_ref[pl.ds(d0, tq), :], v_ref[pl.ds(d0, tq), :], mask)

    o = acc_scr[...] * (1.0 / l_scr[...])
    o_ref[...] = o.T


def _attn_call(kernel, q, k, v, *, batch, t, s, heads, qcols, vcols, tq, tk, scratch, name):
    assert t % tq == 0 and (s - t) % tk == 0 and tq % CHUNK == 0 and (tq == tk or t == tq)
    nq = t // tq
    return pl.pallas_call(
        functools.partial(kernel, tq=tq, tk=tk, past=s - t),
        grid=(batch, heads, nq),
        in_specs=[
            pl.BlockSpec((tq, qcols), lambda b, h, i: (b * nq + i, h)),
            pl.BlockSpec((s, qcols), lambda b, h, i: (b, h)),
            pl.BlockSpec((s, vcols), lambda b, h, i: (b, h)),
        ],
        out_specs=pl.BlockSpec((tq, vcols), lambda b, h, i: (b * nq + i, h)),
        out_shape=jax.ShapeDtypeStruct((batch * t, heads * vcols), F32),
        scratch_shapes=scratch,
        compiler_params=_params("parallel", "parallel", "arbitrary"),
        name=name,
    )(q, k, v)


def _mla_attn(q, k, v, *, batch, t, s, tq, tk):
    scratch = [pltpu.VMEM((1, tq), F32), pltpu.VMEM((1, tq), F32), pltpu.VMEM((D_VA, tq), F32)]
    return _attn_call(_mla_kernel, q, k, v, batch=batch, t=t, s=s, heads=H_MLA, qcols=Q_HEAD_COLS,
                      vcols=D_VA, tq=tq, tk=tk, scratch=scratch, name="mla_attn")


def _sb_kernel(q_ref, k_ref, v_ref, o_ref, c_scr, acc_scr, *, tq, tk, past):
    i = pl.program_id(2)
    q = q_ref[...]
    c_scr[...] = jnp.zeros(c_scr.shape, F32)
    acc_scr[...] = jnp.zeros(acc_scr.shape, F32)

    def update(kb, vb, n, before):
        z = _dot_nt(kb, q)
        ls = jnp.minimum(-z, 0.0) - jnp.log1p(jnp.exp(-jnp.abs(z)))
        lsm = ls if before is None else jnp.where(before, ls, 0.0)
        hi = lsm.astype(BF16)
        lo = (lsm - hi.astype(F32)).astype(BF16)
        later_eq = (lax.broadcasted_iota(jnp.int32, (n, n), 1)
                    >= lax.broadcasted_iota(jnp.int32, (n, n), 0)).astype(BF16)
        incl = _dot(later_eq, hi) + _dot(later_eq, lo)
        c = c_scr[...]
        w = jnp.exp(z + ls + (incl - lsm) + c)
        if before is not None:
            w = jnp.where(before, w, 0.0)
        acc_scr[...] += _dot_tn(vb, w.astype(BF16))
        c_new = c + incl[0:1, :]
        c_scr[...] = c_new
        return jnp.max(c_new)

    q0 = past + i * tq
    d0 = pl.multiple_of(q0, tq)
    before = lax.broadcasted_iota(jnp.int32, (tq, tq), 0) < lax.broadcasted_iota(jnp.int32, (tq, tq), 1)
    top = update(k_ref[pl.ds(d0, tq), :], v_ref[pl.ds(d0, tq), :], tq, before)

    def cond(st):
        j, top = st
        return jnp.logical_and(j >= 0, top > SB_EXP_ZERO)

    def body(st):
        j, _ = st
        r0 = pl.multiple_of(j * tk, tk)
        return j - 1, update(k_ref[pl.ds(r0, tk), :], v_ref[pl.ds(r0, tk), :], tk, None)

    lax.while_loop(cond, body, (q0 // tk - 1, top))
    o_ref[...] = acc_scr[...].T


def _sb_attn(q, k, v, *, batch, t, s, tq, tk):
    scratch = [pltpu.VMEM((1, tq), F32), pltpu.VMEM((D_SB, tq), F32)]
    return _attn_call(_sb_kernel, q, k, v, batch=batch, t=t, s=s, heads=H_SB, qcols=D_SB,
                      vcols=D_SB, tq=tq, tk=tk, scratch=scratch, name="sb_attn")


def _post_kernel(oa_ref, ob_ref, x_ref, mod_ref, goa_ref, gob_ref, gnf_ref, wout_ref, wg_ref, wu_ref, wd_ref,
                 wconv_ref, bconv_ref, cbuf_ref, y_ref, nconv_ref, x1_scr, h_scr, u_scr,
                 *, seg, nseg, d, per_batch):
    i = pl.program_id(0)
    na = oa_ref.shape[1]
    ma = _rms_rows(oa_ref[...], goa_ref[...]).astype(BF16)
    mb = _rms_rows(ob_ref[...], gob_ref[...]).astype(BF16)
    mix = _dot(ma, wout_ref[0:na, :]) + _dot(mb, wout_ref[na:, :])
    for s in range(nseg):
        rows = slice(s * seg, (s + 1) * seg)
        m = mod_ref[s]
        x1 = x_ref[rows, :] + m[:, 2 * d:3 * d] * mix[rows, :]
        x1_scr[rows, :] = x1
        h = _rms_rows(x1, gnf_ref[...]) * (1.0 + m[:, 4 * d:5 * d]) + m[:, 3 * d:4 * d]
        h_scr[rows, :] = h.astype(BF16)
    h = h_scr[...]
    u = _dot(h, wg_ref[...])

    if nseg == 1:
        first = i % per_batch == 0

        @pl.when(first)
        def _():
            u_scr[0, 8 - (CONV_W - 1):8, :] = cbuf_ref[0]

        @pl.when(jnp.logical_not(first))
        def _():
            u_scr[0, 0:8, :] = u_scr[0, seg:seg + 8, :]
    else:
        for s in range(nseg):
            u_scr[s, 8 - (CONV_W - 1):8, :] = cbuf_ref[s]
    for s in range(nseg):
        u_scr[s, 8:8 + seg, :] = u[s * seg:(s + 1) * seg, :]
        nconv_ref[s] = u[(s + 1) * seg - (CONV_W - 1):(s + 1) * seg, :]
    wc = wconv_ref[...]
    parts = []
    for s in range(nseg):
        uc = bconv_ref[...]
        for tap in range(CONV_W):
            lo = 8 - (CONV_W - 1) + tap
            uc = uc + u_scr[s, lo:lo + seg, :] * wc[tap:tap + 1, :]
        parts.append(uc)
    uc = parts[0] if nseg == 1 else jnp.concatenate(parts, axis=0)

    act = (uc * jax.nn.sigmoid(uc) * _dot(h, wu_ref[...])).astype(BF16)
    down = _dot(act, wd_ref[...])
    for s in range(nseg):
        rows = slice(s * seg, (s + 1) * seg)
        y_ref[rows, :] = x1_scr[rows, :] + mod_ref[s][:, 5 * d:6 * d] * down[rows, :]


def _post(oa, ob, x2, mod3, goa, gob, gnf, wout, wg, wu, wd, wconv, bconv, cbuf, *, batch, t, tm_max):
    r, d = x2.shape
    ff = wg.shape[1]
    tm, seg, nseg = _row_tiling(batch, t, tm_max)
    per_batch = max(1, t // tm)
    full = lambda shape: pl.BlockSpec(shape, lambda i: (0,) * len(shape))
    rows = lambda w: pl.BlockSpec((tm, w), lambda i: (i, 0))
    if nseg == 1:
        state_spec = pl.BlockSpec((1, CONV_W - 1, ff), lambda i: (i // per_batch, 0, 0))
    else:
        state_spec = pl.BlockSpec((nseg, CONV_W - 1, ff), lambda i: (i, 0, 0))
    return pl.pallas_call(
        functools.partial(_post_kernel, seg=seg, nseg=nseg, d=d, per_batch=per_batch),
        grid=(r // tm,),
        in_specs=[rows(oa.shape[1]), rows(ob.shape[1]), rows(d), _mod_spec(t, tm, nseg, 6 * d),
                  full((1, oa.shape[1])), full((1, ob.shape[1])), full((1, d)),
                  full(wout.shape), full(wg.shape), full(wu.shape), full(wd.shape),
                  full((CONV_W, ff)), full((1, ff)), state_spec],
        out_specs=[rows(d), state_spec],
        out_shape=(jax.ShapeDtypeStruct((r, d), F32),
                   jax.ShapeDtypeStruct((batch, CONV_W - 1, ff), F32)),
        scratch_shapes=[pltpu.VMEM((tm, d), F32), pltpu.VMEM((tm, d), BF16),
                        pltpu.VMEM((nseg, seg + 8, ff), F32)],
        compiler_params=_params("arbitrary"),
        name="post_ffn",
    )(oa, ob, x2, mod3, goa, gob, gnf, wout, wg, wu, wd, wconv, bconv, cbuf)


def _in_cols():
    qa = H_MLA * (D_NOPE + D_ROPE)
    half = D_ROPE // 2
    cols = []
    for h in range(H_MLA):
        base = h * (D_NOPE + D_ROPE)
        pe = base + D_NOPE
        cols += list(range(base, pe + D_ROPE))
        cols += list(range(pe + half, pe + D_ROPE)) + list(range(pe, pe + half))
    cols += list(range(qa, qa + D_LATENT))
    kp = qa + D_LATENT
    cols += list(range(kp, kp + D_ROPE)) + list(range(kp + half, kp + D_ROPE)) + list(range(kp, kp + half))
    cols += list(range(kp + D_ROPE, kp + D_ROPE + 3 * H_SB * D_SB))
    return jnp.asarray(cols, jnp.int32)


def _rope_gain(g):
    half = D_ROPE // 2
    return jnp.concatenate([g, g[..., half:], g[..., :half]], axis=-1)


def _rope_tables(pos):
    half = D_ROPE // 2
    freqs = ROPE_THETA ** (-jnp.arange(half, dtype=F32) / half)
    ang = pos.astype(F32)[:, None] * freqs[None, :]
    cos, sin = jnp.cos(ang), jnp.sin(ang)
    return jnp.tile(cos, (1, 4)), jnp.concatenate([-sin, sin, sin, -sin], axis=-1)


def kernel(x_prompt, x_sample, c_prompt, c_sample, cache_mla_latent, cache_mla_krope, cache_sb_k, cache_sb_v, state_ffn_conv, w_ada, b_ada, g_norm_mix, g_norm_ffn, w_in, g_kv_latent, g_q_nope, g_q_rope, g_k_nope, g_k_rope, w_uk, w_uv, g_out_mla, g_out_sb, w_out, w_gate, w_up, w_conv, b_conv, w_down):
    depth = w_in.shape[0]
    bp, tp, d = x_prompt.shape
    bs, ts, _ = x_sample.shape
    past = cache_mla_latent.shape[2]
    ff = w_gate.shape[2]
    sb = H_SB * D_SB

    nb = bp + bs
    nb_pad = -(-nb // 8) * 8
    c_all = jnp.concatenate([c_prompt, c_sample, jnp.zeros((nb_pad - nb, d), F32)], axis=0)
    mod = _ada_all(c_all, w_ada, b_ada)

    w_in_p = jnp.take(w_in, _in_cols(), axis=2).astype(BF16)
    w_uk16, w_uv16, w_out16 = w_uk.astype(BF16), w_uv.astype(BF16), w_out.astype(BF16)
    w_gate16, w_up16, w_down16 = w_gate.astype(BF16), w_up.astype(BF16), w_down.astype(BF16)
    gqr, gkr = _rope_gain(g_q_rope), _rope_gain(g_k_rope)

    cos_p, sin_p = _rope_tables(jnp.arange(tp, dtype=jnp.int32))
    cos_p, sin_p = jnp.tile(cos_p, (bp, 1)), jnp.tile(sin_p, (bp, 1))
    cos_s, sin_s = _rope_tables(past + jnp.arange(ts, dtype=jnp.int32))
    cos_s, sin_s = jnp.tile(cos_s, (bs, 1)), jnp.tile(sin_s, (bs, 1))

    def layer(l, x2, mod3, cos, sin, batch, t, p, caches, tq):
        row = lambda a: a[l][None, :]
        qm, lat, kpe, kpe128, qb, kb, vb, kb16, vb16 = _inproj(
            x2, mod3, row(g_norm_mix), w_in_p[l], row(g_q_nope), row(gqr), row(g_kv_latent), row(gkr),
            cos, sin, batch=batch, t=t, tm_max=512)
        s = p + t
        if p:
            past_lat, past_kpe, past_k, past_v, cbuf = caches
            cat = lambda old, new: jnp.concatenate(
                [old, new.reshape(batch, t, new.shape[-1])], axis=1).reshape(batch * s, new.shape[-1])
            lat_all = cat(past_lat, lat)
            kpe_all = cat(jnp.pad(past_kpe, ((0, 0), (0, 0), (0, LANES - D_ROPE))).astype(BF16), kpe128)
            k_all = cat(past_k.reshape(batch, p, sb).astype(BF16), kb16)
            v_all = cat(past_v.reshape(batch, p, sb).astype(BF16), vb16)
        else:
            lat_all, kpe_all, k_all, v_all = lat, kpe128, kb16, vb16
            cbuf = jnp.zeros((batch, CONV_W - 1, ff), F32)
        k_mla, v_mla = _kvup(lat_all, kpe_all, w_uk16[l], w_uv16[l], row(g_k_nope), tm_max=512)
        oa = _mla_attn(qm, k_mla, v_mla, batch=batch, t=t, s=s, tq=tq, tk=256)
        ob = _sb_attn(qb, k_all, v_all, batch=batch, t=t, s=s, tq=tq, tk=256)
        y, nconv = _post(oa, ob, x2, mod3, row(g_out_mla), row(g_out_sb), row(g_norm_ffn), w_out16[l],
                         w_gate16[l], w_up16[l], w_down16[l], w_conv[l], row(b_conv), cbuf,
                         batch=batch, t=t, tm_max=256)
        return (y, lat.reshape(batch, t, D_LATENT), kpe.reshape(batch, t, D_ROPE),
                kb.reshape(batch, t, H_SB, D_SB), vb.reshape(batch, t, H_SB, D_SB), nconv)

    xp = x_prompt.reshape(bp * tp, d)
    xs = x_sample.reshape(bs * ts, d)
    outs_p, outs_s = [], []
    for l in range(depth):
        mod_p = mod[l, :bp].reshape(bp, 1, 6 * d)
        mod_s = mod[l, bp:nb].reshape(bs, 1, 6 * d)
        rp = layer(l, xp, mod_p, cos_p, sin_p, bp, tp, 0, None, min(256, tp))
        rs = layer(l, xs, mod_s, cos_s, sin_s, bs, ts, past,
                   (cache_mla_latent[l], cache_mla_krope[l], cache_sb_k[l], cache_sb_v[l], state_ffn_conv[l]),
                   min(256, ts))
        xp, xs = rp[0], rs[0]
        outs_p.append(rp[1:])
        outs_s.append(rs[1:])

    stack = lambda outs, k: jnp.stack([o[k] for o in outs])
    return ((xp.reshape(bp, tp, d), xs.reshape(bs, ts, d))
            + tuple(stack(outs_p, k) for k in range(5))
            + tuple(stack(outs_s, k) for k in range(5)))
```

```python
import functools
import math

import jax
import jax.numpy as jnp
from jax import lax
from jax.experimental import pallas as pl
from jax.experimental.pallas import tpu as pltpu

F32 = jnp.float32
BF16 = jnp.bfloat16

H_MLA = 4
D_NOPE = 128
D_ROPE = 64
D_VA = 128
D_LATENT = 512
H_SB = 4
D_SB = 128
CONV_W = 3
CHUNK = 64
ROPE_THETA = 10000.0
EPS = 1e-6

LANES = 128
Q_HEAD_COLS = 2 * LANES
MASK_VALUE = -1e30
SB_EXP_ZERO = -105.0
VMEM_LIMIT_BYTES = 56 * 1024 * 1024

MLA_Q_SCALE = math.log2(math.e) * (D_NOPE + D_ROPE) ** -0.5
SB_SCALE = D_SB ** -0.5


def _rms_rows(x, g):
    return x * lax.rsqrt(jnp.mean(x * x, axis=-1, keepdims=True) + EPS) * g


def _dot(a, b):
    return jnp.dot(a, b, preferred_element_type=F32)


def _dot_nt(a, b):
    return lax.dot_general(a, b, (((1,), (1,)), ((), ())), preferred_element_type=F32)


def _dot_tn(a, b):
    return lax.dot_general(a, b, (((0,), (0,)), ((), ())), preferred_element_type=F32)


def _params(*sem):
    return pltpu.CompilerParams(dimension_semantics=sem, vmem_limit_bytes=VMEM_LIMIT_BYTES)


def _ada_kernel(c_ref, w_ref, b_ref, o_ref):
    c = c_ref[...]
    s = (c * jax.nn.sigmoid(c)).astype(BF16)
    o_ref[0] = _dot(s, w_ref[0].astype(BF16)) + b_ref[0]


def _ada_all(c_all, w_ada, b_ada):
    depth, d, n = w_ada.shape
    bp = c_all.shape[0]
    tn = d
    return pl.pallas_call(
        _ada_kernel,
        grid=(depth, n // tn),
        in_specs=[
            pl.BlockSpec((bp, d), lambda l, j: (0, 0)),
            pl.BlockSpec((1, d, tn), lambda l, j: (l, 0, j)),
            pl.BlockSpec((1, 1, tn), lambda l, j: (l, 0, j)),
        ],
        out_specs=pl.BlockSpec((1, bp, tn), lambda l, j: (l, 0, j)),
        out_shape=jax.ShapeDtypeStruct((depth, bp, n), F32),
        compiler_params=_params("parallel", "parallel"),
        name="ada_mod",
    )(c_all, w_ada, b_ada.reshape(depth, 1, n))


C_QM = 0
C_LAT = H_MLA * Q_HEAD_COLS
C_KPE = C_LAT + D_LATENT
C_QB = C_KPE + LANES
C_KB = C_QB + H_SB * D_SB
C_VB = C_KB + H_SB * D_SB
C_END = C_VB + H_SB * D_SB


def _rope_pair(v, g, cos, sin):
    ssq = jnp.sum(v * v, axis=-1, keepdims=True) * (0.5 / D_ROPE)
    vn = v * lax.rsqrt(ssq + EPS) * g
    return vn * cos + pltpu.roll(vn, D_ROPE, 1) * sin


def _inproj_kernel(x_ref, mod_ref, gn_ref, w_ref, gqn_ref, gqr_ref, glat_ref, gkr_ref, cos_ref, sin_ref,
                   qm_ref, lat_ref, kpe_ref, kpe128_ref, qb_ref, kb_ref, vb_ref, kb16_ref, vb16_ref,
                   h_scr, *, seg, nseg, d):
    for s in range(nseg):
        rows = slice(s * seg, (s + 1) * seg)
        m = mod_ref[s]
        h = _rms_rows(x_ref[rows, :], gn_ref[...]) * (1.0 + m[:, d:2 * d]) + m[:, 0:d]
        h_scr[rows, :] = h.astype(BF16)
    h = h_scr[...]
    cos = cos_ref[...]
    sin = sin_ref[...]

    for hd in range(H_MLA):
        c0 = C_QM + hd * Q_HEAD_COLS
        qh = _dot(h, w_ref[:, c0:c0 + Q_HEAD_COLS])
        qn = _rms_rows(qh[:, :D_NOPE], gqn_ref[...]) * MLA_Q_SCALE
        qp = _rope_pair(qh[:, D_NOPE:], gqr_ref[...], cos, sin) * MLA_Q_SCALE
        qm_ref[:, c0:c0 + D_NOPE] = qn.astype(BF16)
        qm_ref[:, c0 + D_NOPE:c0 + Q_HEAD_COLS] = qp.astype(BF16)

    lat_ref[...] = _rms_rows(_dot(h, w_ref[:, C_LAT:C_KPE]), glat_ref[...])

    kp = _rope_pair(_dot(h, w_ref[:, C_KPE:C_QB]), gkr_ref[...], cos, sin)
    kpe_ref[...] = kp[:, :D_ROPE]
    lane = lax.broadcasted_iota(jnp.int32, kp.shape, 1)
    kpe128_ref[...] = jnp.where(lane < D_ROPE, kp, 0.0).astype(BF16)

    qb_ref[...] = (_dot(h, w_ref[:, C_QB:C_KB]) * SB_SCALE).astype(BF16)
    kb = _dot(h, w_ref[:, C_KB:C_VB])
    kb_ref[...] = kb
    kb16_ref[...] = kb.astype(BF16)
    vb = _dot(h, w_ref[:, C_VB:C_END])
    vb_ref[...] = vb
    vb16_ref[...] = vb.astype(BF16)


def _row_tiling(batch, t, tm_max):
    if t >= tm_max:
        assert t % tm_max == 0
        return tm_max, tm_max, 1
    nseg = max(1, min(batch, tm_max // t))
    while batch % nseg:
        nseg -= 1
    return nseg * t, t, nseg


def _mod_spec(t, tm, nseg, width):
    if nseg == 1:
        per = t // tm
        return pl.BlockSpec((1, 1, width), lambda i: (i // per, 0, 0))
    return pl.BlockSpec((nseg, 1, width), lambda i: (i, 0, 0))


def _inproj(x2, mod3, gn, w_p, gqn, gqr, glat, gkr, cos, sin, *, batch, t, tm_max):
    r, d = x2.shape
    tm, seg, nseg = _row_tiling(batch, t, tm_max)
    full = lambda shape: pl.BlockSpec(shape, lambda i: (0,) * len(shape))
    rows = lambda w: pl.BlockSpec((tm, w), lambda i: (i, 0))
    sb = H_SB * D_SB
    out_shapes = (
        jax.ShapeDtypeStruct((r, H_MLA * Q_HEAD_COLS), BF16),
        jax.ShapeDtypeStruct((r, D_LATENT), F32),
        jax.ShapeDtypeStruct((r, D_ROPE), F32),
        jax.ShapeDtypeStruct((r, LANES), BF16),
        jax.ShapeDtypeStruct((r, sb), BF16),
        jax.ShapeDtypeStruct((r, sb), F32),
        jax.ShapeDtypeStruct((r, sb), F32),
        jax.ShapeDtypeStruct((r, sb), BF16),
        jax.ShapeDtypeStruct((r, sb), BF16),
    )
    return pl.pallas_call(
        functools.partial(_inproj_kernel, seg=seg, nseg=nseg, d=d),
        grid=(r // tm,),
        in_specs=[rows(d), _mod_spec(t, tm, nseg, 6 * d), full((1, d)), full(w_p.shape),
                  full((1, D_NOPE)), full((1, LANES)), full((1, D_LATENT)), full((1, LANES)),
                  rows(LANES), rows(LANES)],
        out_specs=[rows(s.shape[1]) for s in out_shapes],
        out_shape=out_shapes,
        scratch_shapes=[pltpu.VMEM((tm, d), BF16)],
        compiler_params=_params("parallel"),
        name="in_proj",
    )(x2, mod3, gn, w_p, gqn, gqr, glat, gkr, cos, sin)


def _kvup_kernel(lat_ref, wuk_ref, wuv_ref, gkn_ref, k_ref, v_ref, *, v_transposed):
    l16 = lat_ref[...].astype(BF16)
    kn = _dot(l16, wuk_ref[...])
    for hd in range(H_MLA):
        cols = slice(hd * D_NOPE, (hd + 1) * D_NOPE)
        k_ref[:, cols] = _rms_rows(kn[:, cols], gkn_ref[...]).astype(BF16)
    if v_transposed:
        v_ref[...] = _dot_nt(wuv_ref[...], l16).astype(BF16)
    else:
        v_ref[...] = _dot(l16, wuv_ref[...]).astype(BF16)


def _kvup(lat_all, wuk, wuv, gkn, *, tm_max, per_batch_t=None):
    r = lat_all.shape[0]
    tm = min(tm_max, r if per_batch_t is None else per_batch_t)
    while r % tm:
        tm //= 2
    full = lambda shape: pl.BlockSpec(shape, lambda i: (0,) * len(shape))
    rows = lambda w: pl.BlockSpec((tm, w), lambda i: (i, 0))
    vcols = H_MLA * D_VA
    if per_batch_t is None:
        v_spec, v_shape = rows(vcols), jax.ShapeDtypeStruct((r, vcols), BF16)
    else:
        per = per_batch_t // tm
        v_spec = pl.BlockSpec((None, vcols, tm), lambda i: (i // per, 0, i % per))
        v_shape = jax.ShapeDtypeStruct((r // per_batch_t, vcols, per_batch_t), BF16)
    return pl.pallas_call(
        functools.partial(_kvup_kernel, v_transposed=per_batch_t is not None),
        grid=(r // tm,),
        in_specs=[rows(D_LATENT), full(wuk.shape), full(wuv.shape), full((1, D_NOPE))],
        out_specs=[rows(H_MLA * D_NOPE), v_spec],
        out_shape=(jax.ShapeDtypeStruct((r, H_MLA * D_NOPE), BF16), v_shape),
        compiler_params=_params("parallel"),
        name="kv_up",
    )(lat_all, wuk, wuv, gkn)


def _mla_kernel(q_ref, kn_ref, kpe_ref, v_ref, o_ref, m_scr, l_scr, acc_scr, *, tq, tk, past):
    i = pl.program_id(1)
    m_scr[...] = jnp.full(m_scr.shape, MASK_VALUE, F32)
    l_scr[...] = jnp.zeros(l_scr.shape, F32)
    acc_scr[...] = jnp.zeros(acc_scr.shape, F32)

    def update(r0, n, mask):
        kpe = kpe_ref[pl.ds(r0, n), :]
        for hd in range(H_MLA):
            kb = jnp.concatenate([kn_ref[pl.ds(r0, n), hd * D_NOPE:(hd + 1) * D_NOPE], kpe], axis=-1)
            s = _dot_nt(kb, q_ref[:, hd * Q_HEAD_COLS:(hd + 1) * Q_HEAD_COLS])
            if mask is not None:
                s = jnp.where(mask, s, MASK_VALUE)
            m_old = m_scr[hd]
            m_new = jnp.maximum(m_old, jnp.max(s, axis=0, keepdims=True))
            alpha = jnp.exp2(m_old - m_new)
            p = jnp.exp2(s - m_new)
            l_scr[hd] = alpha * l_scr[hd] + jnp.sum(p, axis=0, keepdims=True)
            vb = v_ref[pl.ds(r0, n), hd * D_VA:(hd + 1) * D_VA]
            acc_scr[hd] = alpha * acc_scr[hd] + _dot_tn(vb, p.astype(BF16))
            m_scr[hd] = m_new

    q0 = past + i * tq

    def body(j, carry):
        update(pl.multiple_of(j * tk, tk), tk, None)
        return carry

    lax.fori_loop(0, q0 // tk, body, 0)

    if tq > CHUNK:
        kc = lax.broadcasted_iota(jnp.int32, (tq, tq), 0) // CHUNK
        qc = lax.broadcasted_iota(jnp.int32, (tq, tq), 1) // CHUNK
        mask = kc <= qc
    else:
        mask = None
    update(pl.multiple_of(q0, tq), tq, mask)

    for hd in range(H_MLA):
        o = acc_scr[hd] * (1.0 / l_scr[hd])
        o_ref[:, hd * D_VA:(hd + 1) * D_VA] = o.T


def _attn_call(kernel, q, kv, *, batch, t, s, tq, tk, scratch, name):
    assert t % tq == 0 and (s - t) % tk == 0 and tq % CHUNK == 0 and (tq == tk or t == tq)
    nq = t // tq
    vcols = kv[-1].shape[1]
    resident = lambda a: pl.BlockSpec((s, a.shape[1]), lambda b, i: (b, 0), pipeline_mode=pl.Buffered(1))
    return pl.pallas_call(
        functools.partial(kernel, tq=tq, tk=tk, past=s - t),
        grid=(batch, nq),
        in_specs=[pl.BlockSpec((tq, q.shape[1]), lambda b, i: (b * nq + i, 0))] + [resident(a) for a in kv],
        out_specs=pl.BlockSpec((tq, vcols), lambda b, i: (b * nq + i, 0)),
        out_shape=jax.ShapeDtypeStruct((batch * t, vcols), F32),
        scratch_shapes=scratch,
        compiler_params=_params("parallel", "arbitrary"),
        name=name,
    )(q, *kv)


def _mla_attn(q, kn, kpe, v, *, batch, t, s, tq, tk):
    scratch = [pltpu.VMEM((H_MLA, 1, tq), F32), pltpu.VMEM((H_MLA, 1, tq), F32),
               pltpu.VMEM((H_MLA, D_VA, tq), F32)]
    return _attn_call(_mla_kernel, q, (kn, kpe, v), batch=batch, t=t, s=s, tq=tq, tk=tk,
                      scratch=scratch, name="mla_attn")


def _mla_first_kernel(q_ref, kn_ref, kpe_ref, vt_ref, o_ref, qt_scr, m_scr, l_scr, acc_scr, s_scr, p_scr, a_scr,
                      *, tq, tk):
    i = pl.program_id(1)
    nb = tq // tk
    assert nb == 2
    for hd in range(H_MLA):
        qt_scr[hd] = q_ref[:, hd * Q_HEAD_COLS:(hd + 1) * Q_HEAD_COLS].T
    m_scr[...] = jnp.full(m_scr.shape, MASK_VALUE, F32)
    l_scr[...] = jnp.zeros(l_scr.shape, F32)
    acc_scr[...] = jnp.zeros(acc_scr.shape, F32)
    last = i * nb + nb - 1

    def stage_a(b, buf):
        r0 = pl.multiple_of(b * tk, tk)
        kpe = kpe_ref[pl.ds(r0, tk), :]
        for hd in range(H_MLA):
            kb = jnp.concatenate([kn_ref[pl.ds(r0, tk), hd * D_NOPE:(hd + 1) * D_NOPE], kpe], axis=-1)
            s_scr[buf, hd] = _dot(kb, qt_scr[hd])

    def stage_b(buf, own):
        if own is not None:
            kc = lax.broadcasted_iota(jnp.int32, (tk, tq), 0) // CHUNK + own * (tk // CHUNK)
            qc = lax.broadcasted_iota(jnp.int32, (tk, tq), 1) // CHUNK
            mask = kc <= qc
        for hd in range(H_MLA):
            s = s_scr[buf, hd]
            if own is not None:
                s = jnp.where(mask, s, MASK_VALUE)
            m_old = m_scr[hd]
            m_new = jnp.maximum(m_old, jnp.max(s, axis=0, keepdims=True))
            alpha = jnp.exp2(m_old - m_new)
            p = jnp.exp2(s - m_new)
            l_scr[hd] = alpha * l_scr[hd] + jnp.sum(p, axis=0, keepdims=True)
            m_scr[hd] = m_new
            a_scr[buf, hd] = alpha
            p_scr[buf, hd] = p.astype(BF16)

    def stage_c(b, buf):
        r0 = pl.multiple_of(b * tk, tk)
        for hd in range(H_MLA):
            vt = vt_ref[hd * D_VA:(hd + 1) * D_VA, pl.ds(r0, tk)]
            acc_scr[hd] = a_scr[buf, hd] * acc_scr[hd] + _dot(vt, p_scr[buf, hd])

    def step(t, buf, own=None):
        stage_a(t, buf)
        stage_c(t - 2, buf)
        stage_b(1 - buf, own)

    @pl.when(i == 0)
    def _():
        stage_a(0, 0)
        stage_a(1, 1)
        stage_b(0, 0)
        stage_c(0, 0)
        stage_b(1, 1)
        stage_c(1, 1)

    @pl.when(i > 0)
    def _():
        stage_a(0, 0)
        stage_a(1, 1)
        stage_b(0, None)
        step(2, 0)

        def body(k, carry):
            t = 3 + 2 * k
            step(t, 1)
            step(t + 1, 0)
            return carry

        lax.fori_loop(0, i - 1, body, 0)
        step(last, 1, own=0)
        stage_c(last - 1, 0)
        stage_b(1, 1)
        stage_c(last, 1)

    for hd in range(H_MLA):
        o = acc_scr[hd] * (1.0 / l_scr[hd])
        o_ref[:, hd * D_VA:(hd + 1) * D_VA] = o.T


def _mla_attn_first(q, kn, kpe, vt, *, batch, t, tq, tk):
    assert t % tq == 0 and tq == 2 * tk and tk % CHUNK == 0
    nq = t // tq
    vcols = H_MLA * D_VA
    resident = lambda a: pl.BlockSpec((t, a.shape[1]), lambda b, i: (b, 0), pipeline_mode=pl.Buffered(1))
    scratch = [pltpu.VMEM((H_MLA, Q_HEAD_COLS, tq), BF16),
               pltpu.VMEM((H_MLA, 1, tq), F32), pltpu.VMEM((H_MLA, 1, tq), F32),
               pltpu.VMEM((H_MLA, D_VA, tq), F32),
               pltpu.VMEM((2, H_MLA, tk, tq), F32), pltpu.VMEM((2, H_MLA, tk, tq), BF16),
               pltpu.VMEM((2, H_MLA, 1, tq), F32)]
    return pl.pallas_call(
        functools.partial(_mla_first_kernel, tq=tq, tk=tk),
        grid=(batch, nq),
        in_specs=[pl.BlockSpec((tq, q.shape[1]), lambda b, i: (b * nq + i, 0)), resident(kn), resident(kpe),
                  pl.BlockSpec((None, vcols, t), lambda b, i: (b, 0, 0), pipeline_mode=pl.Buffered(1))],
        out_specs=pl.BlockSpec((tq, vcols), lambda b, i: (b * nq + i, 0)),
        out_shape=jax.ShapeDtypeStruct((batch * t, vcols), F32),
        scratch_shapes=scratch,
        compiler_params=_params("parallel", "arbitrary"),
        name="mla_attn_first",
    )(q, kn, kpe, vt)


def _sb_kernel(q_ref, k_ref, v_ref, o_ref, c_scr, acc_scr, *, tq, tk, past):
    i = pl.program_id(1)
    c_scr[...] = jnp.zeros(c_scr.shape, F32)
    acc_scr[...] = jnp.zeros(acc_scr.shape, F32)

    def later_eq(n):
        return (lax.broadcasted_iota(jnp.int32, (n, n), 1)
                >= lax.broadcasted_iota(jnp.int32, (n, n), 0)).astype(BF16)

    def update(r0, n, tri, before):
        heads = [slice(hd * D_SB, (hd + 1) * D_SB) for hd in range(H_SB)]
        zs = [_dot_nt(k_ref[pl.ds(r0, n), cols], q_ref[:, cols]) for cols in heads]
        es, incls = [], []
        for z in zs:
            ls = jnp.minimum(-z, 0.0) - jnp.log1p(jnp.exp(-jnp.abs(z)))
            lsm = ls if before is None else jnp.where(before, ls, 0.0)
            hi = lsm.astype(BF16)
            lo = (lsm - hi.astype(F32)).astype(BF16)
            incl = _dot(tri, hi) + _dot(tri, lo)
            incls.append(incl)
            es.append(z + incl if before is None else z + ls + (incl - lsm))
        top = None
        for hd, cols in enumerate(heads):
            c = c_scr[hd]
            w = jnp.exp(es[hd] + c)
            if before is not None:
                w = jnp.where(before, w, 0.0)
            acc_scr[hd] += _dot_tn(v_ref[pl.ds(r0, n), cols], w.astype(BF16))
            c_new = c + incls[hd][0:1, :]
            c_scr[hd] = c_new
            top_h = jnp.max(c_new)
            top = top_h if top is None else jnp.maximum(top, top_h)
        return top

    q0 = past + i * tq
    before = lax.broadcasted_iota(jnp.int32, (tq, tq), 0) < lax.broadcasted_iota(jnp.int32, (tq, tq), 1)
    top = update(pl.multiple_of(q0, tq), tq, later_eq(tq), before)
    tri_k = later_eq(tk)

    def cond(st):
        j, top = st
        return jnp.logical_and(j >= 0, top > SB_EXP_ZERO)

    def body(st):
        j, _ = st
        return j - 1, update(pl.multiple_of(j * tk, tk), tk, tri_k, None)

    lax.while_loop(cond, body, (q0 // tk - 1, top))
    for hd in range(H_SB):
        o_ref[:, hd * D_SB:(hd + 1) * D_SB] = acc_scr[hd].T


def _sb_attn(q, k, v, *, batch, t, s, tq, tk):
    scratch = [pltpu.VMEM((H_SB, 1, tq), F32), pltpu.VMEM((H_SB, D_SB, tq), F32)]
    return _attn_call(_sb_kernel, q, (k, v), batch=batch, t=t, s=s, tq=tq, tk=tk,
                      scratch=scratch, name="sb_attn")


def _post_kernel(oa_ref, ob_ref, x_ref, mod_ref, goa_ref, gob_ref, gnf_ref, wout_ref, wg_ref, wu_ref, wd_ref,
                 wconv_ref, bconv_ref, cbuf_ref, y_ref, nconv_ref, x1_scr, h_scr, u_scr,
                 *, seg, nseg, d, per_batch):
    i = pl.program_id(0)
    na = oa_ref.shape[1]
    ma = _rms_rows(oa_ref[...], goa_ref[...]).astype(BF16)
    mb = _rms_rows(ob_ref[...], gob_ref[...]).astype(BF16)
    mix = _dot(ma, wout_ref[0:na, :]) + _dot(mb, wout_ref[na:, :])
    for s in range(nseg):
        rows = slice(s * seg, (s + 1) * seg)
        m = mod_ref[s]
        x1 = x_ref[rows, :] + m[:, 2 * d:3 * d] * mix[rows, :]
        x1_scr[rows, :] = x1
        h = _rms_rows(x1, gnf_ref[...]) * (1.0 + m[:, 4 * d:5 * d]) + m[:, 3 * d:4 * d]
        h_scr[rows, :] = h.astype(BF16)
    h = h_scr[...]
    u = _dot(h, wg_ref[...])

    if nseg == 1:
        first = i % per_batch == 0

        @pl.when(first)
        def _():
            u_scr[0, 8 - (CONV_W - 1):8, :] = cbuf_ref[0]

        @pl.when(jnp.logical_not(first))
        def _():
            u_scr[0, 0:8, :] = u_scr[0, seg:seg + 8, :]
    else:
        for s in range(nseg):
            u_scr[s, 8 - (CONV_W - 1):8, :] = cbuf_ref[s]
    for s in range(nseg):
        u_scr[s, 8:8 + seg, :] = u[s * seg:(s + 1) * seg, :]
        nconv_ref[s] = u[(s + 1) * seg - (CONV_W - 1):(s + 1) * seg, :]
    wc = wconv_ref[...]
    parts = []
    for s in range(nseg):
        uc = bconv_ref[...]
        for tap in range(CONV_W):
            lo = 8 - (CONV_W - 1) + tap
            uc = uc + u_scr[s, lo:lo + seg, :] * wc[tap:tap + 1, :]
        parts.append(uc)
    uc = parts[0] if nseg == 1 else jnp.concatenate(parts, axis=0)

    act = (uc * jax.nn.sigmoid(uc) * _dot(h, wu_ref[...])).astype(BF16)
    down = _dot(act, wd_ref[...])
    for s in range(nseg):
        rows = slice(s * seg, (s + 1) * seg)
        y_ref[rows, :] = x1_scr[rows, :] + mod_ref[s][:, 5 * d:6 * d] * down[rows, :]


def _post(oa, ob, x2, mod3, goa, gob, gnf, wout, wg, wu, wd, wconv, bconv, cbuf, *, batch, t, tm_max):
    r, d = x2.shape
    ff = wg.shape[1]
    tm, seg, nseg = _row_tiling(batch, t, tm_max)
    per_batch = max(1, t // tm)
    full = lambda shape: pl.BlockSpec(shape, lambda i: (0,) * len(shape))
    rows = lambda w: pl.BlockSpec((tm, w), lambda i: (i, 0))
    if nseg == 1:
        state_spec = pl.BlockSpec((1, CONV_W - 1, ff), lambda i: (i // per_batch, 0, 0))
    else:
        state_spec = pl.BlockSpec((nseg, CONV_W - 1, ff), lambda i: (i, 0, 0))
    return pl.pallas_call(
        functools.partial(_post_kernel, seg=seg, nseg=nseg, d=d, per_batch=per_batch),
        grid=(r // tm,),
        in_specs=[rows(oa.shape[1]), rows(ob.shape[1]), rows(d), _mod_spec(t, tm, nseg, 6 * d),
                  full((1, oa.shape[1])), full((1, ob.shape[1])), full((1, d)),
                  full(wout.shape), full(wg.shape), full(wu.shape), full(wd.shape),
                  full((CONV_W, ff)), full((1, ff)), state_spec],
        out_specs=[rows(d), state_spec],
        out_shape=(jax.ShapeDtypeStruct((r, d), F32),
                   jax.ShapeDtypeStruct((batch, CONV_W - 1, ff), F32)),
        scratch_shapes=[pltpu.VMEM((tm, d), F32), pltpu.VMEM((tm, d), BF16),
                        pltpu.VMEM((nseg, seg + 8, ff), F32)],
        compiler_params=_params("arbitrary"),
        name="post_ffn",
    )(oa, ob, x2, mod3, goa, gob, gnf, wout, wg, wu, wd, wconv, bconv, cbuf)


def _in_cols():
    qa = H_MLA * (D_NOPE + D_ROPE)
    half = D_ROPE // 2
    cols = []
    for h in range(H_MLA):
        base = h * (D_NOPE + D_ROPE)
        pe = base + D_NOPE
        cols += list(range(base, pe + D_ROPE))
        cols += list(range(pe + half, pe + D_ROPE)) + list(range(pe, pe + half))
    cols += list(range(qa, qa + D_LATENT))
    kp = qa + D_LATENT
    cols += list(range(kp, kp + D_ROPE)) + list(range(kp + half, kp + D_ROPE)) + list(range(kp, kp + half))
    cols += list(range(kp + D_ROPE, kp + D_ROPE + 3 * H_SB * D_SB))
    return jnp.asarray(cols, jnp.int32)


def _rope_gain(g):
    half = D_ROPE // 2
    return jnp.concatenate([g, g[..., half:], g[..., :half]], axis=-1)


def _rope_tables(pos):
    half = D_ROPE // 2
    freqs = ROPE_THETA ** (-jnp.arange(half, dtype=F32) / half)
    ang = pos.astype(F32)[:, None] * freqs[None, :]
    cos, sin = jnp.cos(ang), jnp.sin(ang)
    return jnp.tile(cos, (1, 4)), jnp.concatenate([-sin, sin, sin, -sin], axis=-1)


def kernel(x_prompt, x_sample, c_prompt, c_sample, cache_mla_latent, cache_mla_krope, cache_sb_k, cache_sb_v, state_ffn_conv, w_ada, b_ada, g_norm_mix, g_norm_ffn, w_in, g_kv_latent, g_q_nope, g_q_rope, g_k_nope, g_k_rope, w_uk, w_uv, g_out_mla, g_out_sb, w_out, w_gate, w_up, w_conv, b_conv, w_down):
    depth = w_in.shape[0]
    bp, tp, d = x_prompt.shape
    bs, ts, _ = x_sample.shape
    past = cache_mla_latent.shape[2]
    ff = w_gate.shape[2]
    sb = H_SB * D_SB

    nb = bp + bs
    nb_pad = -(-nb // 8) * 8
    c_all = jnp.concatenate([c_prompt, c_sample, jnp.zeros((nb_pad - nb, d), F32)], axis=0)
    mod = _ada_all(c_all, w_ada, b_ada)

    w_in_p = jnp.take(w_in, _in_cols(), axis=2).astype(BF16)
    w_uk16, w_uv16, w_out16 = w_uk.astype(BF16), w_uv.astype(BF16), w_out.astype(BF16)
    w_uvt16 = jnp.swapaxes(w_uv16, 1, 2)
    w_gate16, w_up16, w_down16 = w_gate.astype(BF16), w_up.astype(BF16), w_down.astype(BF16)
    gqr, gkr = _rope_gain(g_q_rope), _rope_gain(g_k_rope)

    cos_p, sin_p = _rope_tables(jnp.arange(tp, dtype=jnp.int32))
    cos_p, sin_p = jnp.tile(cos_p, (bp, 1)), jnp.tile(sin_p, (bp, 1))
    cos_s, sin_s = _rope_tables(past + jnp.arange(ts, dtype=jnp.int32))
    cos_s, sin_s = jnp.tile(cos_s, (bs, 1)), jnp.tile(sin_s, (bs, 1))

    def layer(l, x2, mod3, cos, sin, batch, t, p, caches, tq):
        row = lambda a: a[l][None, :]
        qm, lat, kpe, kpe128, qb, kb, vb, kb16, vb16 = _inproj(
            x2, mod3, row(g_norm_mix), w_in_p[l], row(g_q_nope), row(gqr), row(g_kv_latent), row(gkr),
            cos, sin, batch=batch, t=t, tm_max=512)
        s = p + t
        if p:
            past_lat, past_kpe, past_k, past_v, cbuf = caches
            cat = lambda old, new: jnp.concatenate(
                [old, new.reshape(batch, t, new.shape[-1])], axis=1).reshape(batch * s, new.shape[-1])
            lat_all = cat(past_lat, lat)
            kpe_all = cat(jnp.pad(past_kpe, ((0, 0), (0, 0), (0, LANES - D_ROPE))).astype(BF16), kpe128)
            k_all = cat(past_k.reshape(batch, p, sb).astype(BF16), kb16)
            v_all = cat(past_v.reshape(batch, p, sb).astype(BF16), vb16)
        else:
            lat_all, kpe_all, k_all, v_all = lat, kpe128, kb16, vb16
            cbuf = jnp.zeros((batch, CONV_W - 1, ff), F32)
        if p == 0 and t % 512 == 0:
            kn_all, vt_mla = _kvup(lat_all, w_uk16[l], w_uvt16[l], row(g_k_nope), tm_max=512, per_batch_t=t)
            oa = _mla_attn_first(qm, kn_all, kpe_all, vt_mla, batch=batch, t=t, tq=512, tk=256)
        else:
            kn_all, v_mla = _kvup(lat_all, w_uk16[l], w_uv16[l], row(g_k_nope), tm_max=512)
            oa = _mla_attn(qm, kn_all, kpe_all, v_mla, batch=batch, t=t, s=s, tq=tq, tk=256)
        ob = _sb_attn(qb, k_all, v_all, batch=batch, t=t, s=s, tq=tq, tk=256)
        y, nconv = _post(oa, ob, x2, mod3, row(g_out_mla), row(g_out_sb), row(g_norm_ffn), w_out16[l],
                         w_gate16[l], w_up16[l], w_down16[l], w_conv[l], row(b_conv), cbuf,
                         batch=batch, t=t, tm_max=256)
        return (y, lat.reshape(batch, t, D_LATENT), kpe.reshape(batch, t, D_ROPE),
                kb.reshape(batch, t, H_SB, D_SB), vb.reshape(batch, t, H_SB, D_SB), nconv)

    xp = x_prompt.reshape(bp * tp, d)
    xs = x_sample.reshape(bs * ts, d)
    outs_p, outs_s = [], []
    for l in range(depth):
        mod_p = mod[l, :bp].reshape(bp, 1, 6 * d)
        mod_s = mod[l, bp:nb].reshape(bs, 1, 6 * d)
        rp = layer(l, xp, mod_p, cos_p, sin_p, bp, tp, 0, None, min(256, tp))
        rs = layer(l, xs, mod_s, cos_s, sin_s, bs, ts, past,
                   (cache_mla_latent[l], cache_mla_krope[l], cache_sb_k[l], cache_sb_v[l], state_ffn_conv[l]),
                   min(256, ts))
        xp, xs = rp[0], rs[0]
        outs_p.append(rp[1:])
        outs_s.append(rs[1:])

    stack = lambda outs, k: jnp.stack([o[k] for o in outs])
    return ((xp.reshape(bp, tp, d), xs.reshape(bs, ts, d))
            + tuple(stack(outs_p, k) for k in range(5))
            + tuple(stack(outs_s, k) for k in range(5)))
```

```python
import functools
import math

import jax
import jax.numpy as jnp
from jax import lax
from jax.experimental import pallas as pl
from jax.experimental.pallas import tpu as pltpu

F32 = jnp.float32
BF16 = jnp.bfloat16

H_MLA = 4
D_NOPE = 128
D_ROPE = 64
D_VA = 128
D_LATENT = 512
H_SB = 4
D_SB = 128
CONV_W = 3
CHUNK = 64
ROPE_THETA = 10000.0
EPS = 1e-6

LANES = 128
MXU_COLS = 256
Q_HEAD_COLS = 2 * LANES
MASK_VALUE = -1e30
FF_CHUNKS = 2
ONES_ROWS = 16
VT_ROWS = D_VA + ONES_ROWS
SB_EXP_ZERO = -105.0
VMEM_LIMIT_BYTES = 56 * 1024 * 1024

MLA_Q_SCALE = math.log2(math.e) * (D_NOPE + D_ROPE) ** -0.5
SB_SCALE = D_SB ** -0.5


def _rms_rows(x, g):
    return x * lax.rsqrt(jnp.mean(x * x, axis=-1, keepdims=True) + EPS) * g


def _dot(a, b):
    return jnp.dot(a, b, preferred_element_type=F32)


def _dot_nt(a, b):
    return lax.dot_general(a, b, (((1,), (1,)), ((), ())), preferred_element_type=F32)


def _dot_tn(a, b):
    return lax.dot_general(a, b, (((0,), (0,)), ((), ())), preferred_element_type=F32)


def _params(*sem):
    return pltpu.CompilerParams(dimension_semantics=sem, vmem_limit_bytes=VMEM_LIMIT_BYTES)


def _ada_kernel(c_ref, w_ref, b_ref, o_ref):
    c = c_ref[...]
    s = (c * jax.nn.sigmoid(c)).astype(BF16)
    o_ref[0] = _dot(s, w_ref[0].astype(BF16)) + b_ref[0]


def _ada_all(c_all, w_ada, b_ada):
    depth, d, n = w_ada.shape
    bp = c_all.shape[0]
    tn = d
    return pl.pallas_call(
        _ada_kernel,
        grid=(depth, n // tn),
        in_specs=[
            pl.BlockSpec((bp, d), lambda l, j: (0, 0)),
            pl.BlockSpec((1, d, tn), lambda l, j: (l, 0, j)),
            pl.BlockSpec((1, 1, tn), lambda l, j: (l, 0, j)),
        ],
        out_specs=pl.BlockSpec((1, bp, tn), lambda l, j: (l, 0, j)),
        out_shape=jax.ShapeDtypeStruct((depth, bp, n), F32),
        compiler_params=_params("parallel", "parallel"),
        name="ada_mod",
    )(c_all, w_ada, b_ada.reshape(depth, 1, n))


C_QM = 0
C_LAT = H_MLA * Q_HEAD_COLS
C_KPE = C_LAT + D_LATENT
C_QB = C_KPE + LANES
C_KB = C_QB + H_SB * D_SB
C_VB = C_KB + H_SB * D_SB
C_END = C_VB + H_SB * D_SB


def _rope_pair(v, g, cos, sin):
    ssq = jnp.sum(v * v, axis=-1, keepdims=True) * (0.5 / D_ROPE)
    vn = v * lax.rsqrt(ssq + EPS) * g
    return vn * cos + pltpu.roll(vn, D_ROPE, 1) * sin


def _inproj_kernel(x_ref, mod_ref, gn_ref, w_ref, gqn_ref, gqr_ref, glat_ref, gkr_ref, cos_ref, sin_ref,
                   qm_ref, lat_ref, kpe_ref, kpe128_ref, qb_ref, kb_ref, vb_ref, kb16_ref, vb16_ref,
                   h_scr, *, seg, nseg, d):
    for s in range(nseg):
        rows = slice(s * seg, (s + 1) * seg)
        m = mod_ref[s]
        h = _rms_rows(x_ref[rows, :], gn_ref[...]) * (1.0 + m[:, d:2 * d]) + m[:, 0:d]
        h_scr[rows, :] = h.astype(BF16)
    h = h_scr[...]
    cos = cos_ref[...]
    sin = sin_ref[...]

    for hd in range(H_MLA):
        c0 = C_QM + hd * Q_HEAD_COLS
        qh = _dot(h, w_ref[:, c0:c0 + Q_HEAD_COLS])
        qn = _rms_rows(qh[:, :D_NOPE], gqn_ref[...]) * MLA_Q_SCALE
        qp = _rope_pair(qh[:, D_NOPE:], gqr_ref[...], cos, sin) * MLA_Q_SCALE
        qm_ref[:, c0:c0 + D_NOPE] = qn.astype(BF16)
        qm_ref[:, c0 + D_NOPE:c0 + Q_HEAD_COLS] = qp.astype(BF16)

    lat_ref[...] = _rms_rows(_dot(h, w_ref[:, C_LAT:C_KPE]), glat_ref[...])

    kp = _rope_pair(_dot(h, w_ref[:, C_KPE:C_QB]), gkr_ref[...], cos, sin)
    kpe_ref[...] = kp[:, :D_ROPE]
    lane = lax.broadcasted_iota(jnp.int32, kp.shape, 1)
    kpe128_ref[...] = jnp.where(lane < D_ROPE, kp, 0.0).astype(BF16)

    qb_ref[...] = (_dot(h, w_ref[:, C_QB:C_KB]) * SB_SCALE).astype(BF16)
    kb = _dot(h, w_ref[:, C_KB:C_VB])
    kb_ref[...] = kb
    kb16_ref[...] = kb.astype(BF16)
    vb = _dot(h, w_ref[:, C_VB:C_END])
    vb_ref[...] = vb
    vb16_ref[...] = vb.astype(BF16)


def _row_tiling(batch, t, tm_max):
    if t >= tm_max:
        assert t % tm_max == 0
        return tm_max, tm_max, 1
    nseg = max(1, min(batch, tm_max // t))
    while batch % nseg:
        nseg -= 1
    return nseg * t, t, nseg


def _mod_spec(t, tm, nseg, width):
    if nseg == 1:
        per = t // tm
        return pl.BlockSpec((1, 1, width), lambda i: (i // per, 0, 0))
    return pl.BlockSpec((nseg, 1, width), lambda i: (i, 0, 0))


def _inproj(x2, mod3, gn, w_p, gqn, gqr, glat, gkr, cos, sin, *, batch, t, tm_max):
    r, d = x2.shape
    tm, seg, nseg = _row_tiling(batch, t, tm_max)
    full = lambda shape: pl.BlockSpec(shape, lambda i: (0,) * len(shape))
    rows = lambda w: pl.BlockSpec((tm, w), lambda i: (i, 0))
    sb = H_SB * D_SB
    out_shapes = (
        jax.ShapeDtypeStruct((r, H_MLA * Q_HEAD_COLS), BF16),
        jax.ShapeDtypeStruct((r, D_LATENT), F32),
        jax.ShapeDtypeStruct((r, D_ROPE), F32),
        jax.ShapeDtypeStruct((r, LANES), BF16),
        jax.ShapeDtypeStruct((r, sb), BF16),
        jax.ShapeDtypeStruct((r, sb), F32),
        jax.ShapeDtypeStruct((r, sb), F32),
        jax.ShapeDtypeStruct((r, sb), BF16),
        jax.ShapeDtypeStruct((r, sb), BF16),
    )
    return pl.pallas_call(
        functools.partial(_inproj_kernel, seg=seg, nseg=nseg, d=d),
        grid=(r // tm,),
        in_specs=[rows(d), _mod_spec(t, tm, nseg, 6 * d), full((1, d)), full(w_p.shape),
                  full((1, D_NOPE)), full((1, LANES)), full((1, D_LATENT)), full((1, LANES)),
                  rows(LANES), rows(LANES)],
        out_specs=[rows(s.shape[1]) for s in out_shapes],
        out_shape=out_shapes,
        scratch_shapes=[pltpu.VMEM((tm, d), BF16)],
        compiler_params=_params("parallel"),
        name="in_proj",
    )(x2, mod3, gn, w_p, gqn, gqr, glat, gkr, cos, sin)


def _kvup_kernel(lat_ref, wuk_ref, wuv_ref, gkn_ref, k_ref, v_ref, *, v_transposed):
    l16 = lat_ref[...].astype(BF16)
    kn = _dot(l16, wuk_ref[...])
    for hd in range(H_MLA):
        cols = slice(hd * D_NOPE, (hd + 1) * D_NOPE)
        k_ref[:, cols] = _rms_rows(kn[:, cols], gkn_ref[...]).astype(BF16)
    if v_transposed:
        vt = _dot_nt(wuv_ref[...], l16).astype(BF16)
        ones = jnp.ones((ONES_ROWS, vt.shape[1]), BF16)
        for hd in range(H_MLA):
            v_ref[hd * VT_ROWS:hd * VT_ROWS + D_VA, :] = vt[hd * D_VA:(hd + 1) * D_VA, :]
            v_ref[hd * VT_ROWS + D_VA:(hd + 1) * VT_ROWS, :] = ones
    else:
        v_ref[...] = _dot(l16, wuv_ref[...]).astype(BF16)


def _kvup(lat_all, wuk, wuv, gkn, *, tm_max, per_batch_t=None, row0=0, n_rows=None):
    r = lat_all.shape[0] if n_rows is None else n_rows
    tm = min(tm_max, r if per_batch_t is None else per_batch_t)
    while r % tm or row0 % tm:
        tm //= 2
    off = row0 // tm
    full = lambda shape: pl.BlockSpec(shape, lambda i: (0,) * len(shape))
    rows = lambda w: pl.BlockSpec((tm, w), lambda i: (i, 0))
    vcols = H_MLA * D_VA
    if per_batch_t is None:
        v_spec, v_shape = rows(vcols), jax.ShapeDtypeStruct((r, vcols), BF16)
    else:
        per = per_batch_t // tm
        v_spec = pl.BlockSpec((None, H_MLA * VT_ROWS, tm), lambda i: (i // per, 0, i % per))
        v_shape = jax.ShapeDtypeStruct((r // per_batch_t, H_MLA * VT_ROWS, per_batch_t), BF16)
    return pl.pallas_call(
        functools.partial(_kvup_kernel, v_transposed=per_batch_t is not None),
        grid=(r // tm,),
        in_specs=[pl.BlockSpec((tm, D_LATENT), lambda i: (i + off, 0)),
                  full(wuk.shape), full(wuv.shape), full((1, D_NOPE))],
        out_specs=[rows(H_MLA * D_NOPE), v_spec],
        out_shape=(jax.ShapeDtypeStruct((r, H_MLA * D_NOPE), BF16), v_shape),
        compiler_params=_params("parallel"),
        name="kv_up",
    )(lat_all, wuk, wuv, gkn)


def _mla_cached_kernel(q_ref, knp_ref, kpep_ref, vp_ref, knn_ref, kpen_ref, vn_ref, o_ref,
                       m_scr, l_scr, acc_scr, *, t, tk, past):
    m_scr[...] = jnp.full(m_scr.shape, MASK_VALUE, F32)
    l_scr[...] = jnp.zeros(l_scr.shape, F32)
    acc_scr[...] = jnp.zeros(acc_scr.shape, F32)
    qk = D_NOPE + D_ROPE

    def update(kn, kpe, v, mask):
        ss = []
        for hd in range(H_MLA):
            kb = jnp.concatenate([kn[:, hd * D_NOPE:(hd + 1) * D_NOPE], kpe], axis=-1)
            ss.append(_dot_nt(kb, q_ref[:, hd * Q_HEAD_COLS:hd * Q_HEAD_COLS + qk]))
        for hd, s in enumerate(ss):
            if mask is not None:
                s = jnp.where(mask, s, MASK_VALUE)
            m_old = m_scr[hd]
            m_new = jnp.maximum(m_old, jnp.max(s, axis=0, keepdims=True))
            alpha = jnp.exp2(m_old - m_new)
            p = jnp.exp2(s - m_new)
            l_scr[hd] = alpha * l_scr[hd] + jnp.sum(p, axis=0, keepdims=True)
            acc_scr[hd] = alpha * acc_scr[hd] + _dot_tn(v[:, hd * D_VA:(hd + 1) * D_VA], p.astype(BF16))
            m_scr[hd] = m_new

    def body(j, carry):
        rows = pl.ds(pl.multiple_of(j * tk, tk), tk)
        update(knp_ref[rows, :], kpep_ref[rows, :].astype(BF16), vp_ref[rows, :], None)
        return carry

    lax.fori_loop(0, past // tk, body, 0)

    if t > CHUNK:
        kc = lax.broadcasted_iota(jnp.int32, (t, t), 0) // CHUNK
        qc = lax.broadcasted_iota(jnp.int32, (t, t), 1) // CHUNK
        mask = kc <= qc
    else:
        mask = None
    update(knn_ref[...], kpen_ref[:, 0:D_ROPE], vn_ref[...], mask)

    for hd in range(H_MLA):
        o = acc_scr[hd] * (1.0 / l_scr[hd])
        o_ref[:, hd * D_VA:(hd + 1) * D_VA] = o.T


def _mla_attn_cached(q, knp, kpe_cache, vp, knn, kpen, vn, *, layer, batch, t, past, tk):
    assert past % tk == 0 and past % CHUNK == 0 and t % CHUNK == 0
    vcols = H_MLA * D_VA
    new = lambda a: pl.BlockSpec((t, a.shape[1]), lambda b: (b, 0))
    old = lambda a: pl.BlockSpec((past, a.shape[1]), lambda b: (b, 0))
    scratch = [pltpu.VMEM((H_MLA, 1, t), F32), pltpu.VMEM((H_MLA, 1, t), F32), pltpu.VMEM((H_MLA, D_VA, t), F32)]
    return pl.pallas_call(
        functools.partial(_mla_cached_kernel, t=t, tk=tk, past=past),
        grid=(batch,),
        in_specs=[new(q), old(knp),
                  pl.BlockSpec((None, None, past, D_ROPE), lambda b: (layer, b, 0, 0)),
                  old(vp), new(knn), new(kpen), new(vn)],
        out_specs=pl.BlockSpec((t, vcols), lambda b: (b, 0)),
        out_shape=jax.ShapeDtypeStruct((batch * t, vcols), F32),
        scratch_shapes=scratch,
        compiler_params=_params("parallel"),
        name="mla_attn_cached",
    )(q, knp, kpe_cache, vp, knn, kpen, vn)


def _mla_first_kernel(q_ref, kn_ref, kpe_ref, vt_ref, o_ref, qt_scr, m_scr, acc_scr, s_scr, p_scr, a_scr,
                      *, tq, tk):
    i = pl.program_id(1)
    nb = tq // tk
    assert nb == 2
    for hd in range(H_MLA):
        qt_scr[hd] = q_ref[:, hd * Q_HEAD_COLS:(hd + 1) * Q_HEAD_COLS].T
    m_scr[...] = jnp.full(m_scr.shape, MASK_VALUE, F32)
    acc_scr[...] = jnp.zeros(acc_scr.shape, F32)
    last = i * nb + nb - 1

    def stage_a(b, buf):
        r0 = pl.multiple_of(b * tk, tk)
        kpe = kpe_ref[pl.ds(r0, tk), :]
        for hd in range(H_MLA):
            kb = jnp.concatenate([kn_ref[pl.ds(r0, tk), hd * D_NOPE:(hd + 1) * D_NOPE], kpe], axis=-1)
            s_scr[buf, hd] = _dot(kb, qt_scr[hd])

    def stage_b(buf, own):
        if own is not None:
            kc = lax.broadcasted_iota(jnp.int32, (tk, tq), 0) // CHUNK + own * (tk // CHUNK)
            qc = lax.broadcasted_iota(jnp.int32, (tk, tq), 1) // CHUNK
            mask = kc <= qc
        for hd in range(H_MLA):
            s = s_scr[buf, hd]
            if own is not None:
                s = jnp.where(mask, s, MASK_VALUE)
            m_old = m_scr[hd]
            m_new = jnp.maximum(m_old, jnp.max(s, axis=0, keepdims=True))
            alpha = jnp.exp2(m_old - m_new)
            p = jnp.exp2(s - m_new)
            m_scr[hd] = m_new
            a_scr[buf, hd] = alpha
            p_scr[buf, hd] = p.astype(BF16)

    def stage_c(b, buf):
        r0 = pl.multiple_of(b * tk, tk)
        for hd in range(H_MLA):
            vt = vt_ref[hd * VT_ROWS:(hd + 1) * VT_ROWS, pl.ds(r0, tk)]
            acc_scr[hd] = a_scr[buf, hd] * acc_scr[hd] + _dot(vt, p_scr[buf, hd])

    def step(t, buf, own=None):
        stage_a(t, buf)
        stage_c(t - 2, buf)
        stage_b(1 - buf, own)

    @pl.when(i == 0)
    def _():
        stage_a(0, 0)
        stage_a(1, 1)
        stage_b(0, 0)
        stage_c(0, 0)
        stage_b(1, 1)
        stage_c(1, 1)

    @pl.when(i > 0)
    def _():
        stage_a(0, 0)
        stage_a(1, 1)
        stage_b(0, None)
        step(2, 0)

        def body(k, carry):
            t = 3 + 2 * k
            step(t, 1)
            step(t + 1, 0)
            return carry

        lax.fori_loop(0, i - 1, body, 0)
        step(last, 1, own=0)
        stage_c(last - 1, 0)
        stage_b(1, 1)
        stage_c(last, 1)

    for hd in range(H_MLA):
        o = acc_scr[hd, 0:D_VA, :] * (1.0 / acc_scr[hd, D_VA:D_VA + 1, :])
        o_ref[:, hd * D_VA:(hd + 1) * D_VA] = o.T


def _mla_attn_first(q, kn, kpe, vt, *, batch, t, tq, tk):
    assert t % tq == 0 and tq == 2 * tk and tk % CHUNK == 0
    nq = t // tq
    vcols = H_MLA * D_VA
    resident = lambda a: pl.BlockSpec((t, a.shape[1]), lambda b, i: (b, 0), pipeline_mode=pl.Buffered(1))
    scratch = [pltpu.VMEM((H_MLA, Q_HEAD_COLS, tq), BF16),
               pltpu.VMEM((H_MLA, 1, tq), F32),
               pltpu.VMEM((H_MLA, VT_ROWS, tq), F32),
               pltpu.VMEM((2, H_MLA, tk, tq), F32), pltpu.VMEM((2, H_MLA, tk, tq), BF16),
               pltpu.VMEM((2, H_MLA, 1, tq), F32)]
    return pl.pallas_call(
        functools.partial(_mla_first_kernel, tq=tq, tk=tk),
        grid=(batch, nq),
        in_specs=[pl.BlockSpec((tq, q.shape[1]), lambda b, i: (b * nq + i, 0)), resident(kn), resident(kpe),
                  pl.BlockSpec((None, H_MLA * VT_ROWS, t), lambda b, i: (b, 0, 0), pipeline_mode=pl.Buffered(1))],
        out_specs=pl.BlockSpec((tq, vcols), lambda b, i: (b * nq + i, 0)),
        out_shape=jax.ShapeDtypeStruct((batch * t, vcols), F32),
        scratch_shapes=scratch,
        compiler_params=_params("parallel", "arbitrary"),
        name="mla_attn_first",
    )(q, kn, kpe, vt)


def _sb_kernel(q_ref, *refs, tq, tk, past):
    if past:
        kp_ref, vp_ref, k_ref, v_ref, o_ref, c_scr, acc_scr = refs
    else:
        k_ref, v_ref, o_ref, c_scr, acc_scr = refs
    i = pl.program_id(1)
    c_scr[...] = jnp.zeros(c_scr.shape, F32)
    acc_scr[...] = jnp.zeros(acc_scr.shape, F32)
    heads = [slice(hd * D_SB, (hd + 1) * D_SB) for hd in range(H_SB)]

    def later_eq(n):
        return (lax.broadcasted_iota(jnp.int32, (n, n), 1)
                >= lax.broadcasted_iota(jnp.int32, (n, n), 0)).astype(BF16)

    def new_rows(r0, n):
        return ([k_ref[pl.ds(r0, n), cols] for cols in heads], [v_ref[pl.ds(r0, n), cols] for cols in heads])

    def cached_rows(r0, n):
        return ([kp_ref[pl.ds(r0, n), hd, :].astype(BF16) for hd in range(H_SB)],
                [vp_ref[pl.ds(r0, n), hd, :].astype(BF16) for hd in range(H_SB)])

    def update(kv, n, tri, before):
        ks, vs = kv
        zs = [_dot_nt(ks[hd], q_ref[:, cols]) for hd, cols in enumerate(heads)]
        es, incls = [], []
        for z in zs:
            ls = jnp.minimum(-z, 0.0) - jnp.log1p(jnp.exp(-jnp.abs(z)))
            lsm = ls if before is None else jnp.where(before, ls, 0.0)
            hi = lsm.astype(BF16)
            lo = (lsm - hi.astype(F32)).astype(BF16)
            incl = _dot(tri, hi) + _dot(tri, lo)
            incls.append(incl)
            es.append(z + incl if before is None else z + ls + (incl - lsm))
        top = None
        for hd in range(H_SB):
            c = c_scr[hd]
            w = jnp.exp(es[hd] + c)
            if before is not None:
                w = jnp.where(before, w, 0.0)
            acc_scr[hd] += _dot_tn(vs[hd], w.astype(BF16))
            c_new = c + incls[hd][0:1, :]
            c_scr[hd] = c_new
            top_h = jnp.max(c_new)
            top = top_h if top is None else jnp.maximum(top, top_h)
        return top

    own0 = i * tq
    before = lax.broadcasted_iota(jnp.int32, (tq, tq), 0) < lax.broadcasted_iota(jnp.int32, (tq, tq), 1)
    top = update(new_rows(pl.multiple_of(own0, tq), tq), tq, later_eq(tq), before)
    tri_k = later_eq(tk)
    older = cached_rows if past else new_rows

    def cond(st):
        j, top = st
        return jnp.logical_and(j >= 0, top > SB_EXP_ZERO)

    def body(st):
        j, _ = st
        return j - 1, update(older(pl.multiple_of(j * tk, tk), tk), tk, tri_k, None)

    lax.while_loop(cond, body, ((past + own0) // tk - 1, top))
    for hd in range(H_SB):
        o_ref[:, hd * D_SB:(hd + 1) * D_SB] = acc_scr[hd].T


def _sb_attn(q, k, v, *, batch, t, tq, tk, past=0, k_cache=None, v_cache=None, layer=0):
    assert t % tq == 0 and past % tk == 0 and (tq == tk if past == 0 else t == tq)
    nq = t // tq
    cols = H_SB * D_SB
    q_spec = pl.BlockSpec((tq, cols), lambda b, i: (b * nq + i, 0))
    if past:
        cache = pl.BlockSpec((None, None, past, H_SB, D_SB), lambda b, i: (layer, b, 0, 0, 0))
        new = pl.BlockSpec((t, cols), lambda b, i: (b, 0))
        in_specs, operands = [q_spec, cache, cache, new, new], (q, k_cache, v_cache, k, v)
    else:
        resident = pl.BlockSpec((t, cols), lambda b, i: (b, 0), pipeline_mode=pl.Buffered(1))
        in_specs, operands = [q_spec, resident, resident], (q, k, v)
    return pl.pallas_call(
        functools.partial(_sb_kernel, tq=tq, tk=tk, past=past),
        grid=(batch, nq),
        in_specs=in_specs,
        out_specs=q_spec,
        out_shape=jax.ShapeDtypeStruct((batch * t, cols), F32),
        scratch_shapes=[pltpu.VMEM((H_SB, 1, tq), F32), pltpu.VMEM((H_SB, D_SB, tq), F32)],
        compiler_params=_params("parallel", "arbitrary"),
        name="sb_attn",
    )(*operands)


def _post_kernel(oa_ref, ob_ref, x_ref, mod_ref, goa_ref, gob_ref, gnf_ref, wout_ref, wg_ref, wu_ref, wd_ref,
                 wconv_ref, bconv_ref, cbuf_ref, y_ref, nconv_ref, x1_scr, h_scr, u_scr,
                 *, seg, nseg, d, per_batch):
    i = pl.program_id(0)
    na = oa_ref.shape[1]
    ma = _rms_rows(oa_ref[...], goa_ref[...]).astype(BF16)
    mb = _rms_rows(ob_ref[...], gob_ref[...]).astype(BF16)
    mix = _dot(ma, wout_ref[0:na, :]) + _dot(mb, wout_ref[na:, :])
    for s in range(nseg):
        rows = slice(s * seg, (s + 1) * seg)
        m = mod_ref[s]
        x1 = x_ref[rows, :] + m[:, 2 * d:3 * d] * mix[rows, :]
        x1_scr[rows, :] = x1
        h = _rms_rows(x1, gnf_ref[...]) * (1.0 + m[:, 4 * d:5 * d]) + m[:, 3 * d:4 * d]
        h_scr[rows, :] = h.astype(BF16)
    h = h_scr[...]

    if nseg == 1:
        first = i % per_batch == 0

        @pl.when(first)
        def _():
            u_scr[0, 8 - (CONV_W - 1):8, :] = cbuf_ref[0]

        @pl.when(jnp.logical_not(first))
        def _():
            u_scr[0, 0:8, :] = u_scr[0, seg:seg + 8, :]
    else:
        for s in range(nseg):
            u_scr[s, 8 - (CONV_W - 1):8, :] = cbuf_ref[s]

    ff = wg_ref.shape[1]
    bounds = [ff * c // FF_CHUNKS // MXU_COLS * MXU_COLS for c in range(FF_CHUNKS)] + [ff]
    chunks = [slice(bounds[c], bounds[c + 1]) for c in range(FF_CHUNKS)]
    us = [_dot(h, wg_ref[:, c]) for c in chunks]
    ups = [_dot(h, wu_ref[:, c]) for c in chunks]
    down = None
    for c, u, up in zip(chunks, us, ups):
        for s in range(nseg):
            u_scr[s, 8:8 + seg, c] = u[s * seg:(s + 1) * seg, :]
            nconv_ref[s, :, c] = u[(s + 1) * seg - (CONV_W - 1):(s + 1) * seg, :]
        parts = []
        for s in range(nseg):
            uc = bconv_ref[:, c]
            for tap in range(CONV_W):
                lo = 8 - (CONV_W - 1) + tap
                uc = uc + u_scr[s, lo:lo + seg, c] * wconv_ref[tap:tap + 1, c]
            parts.append(uc)
        uc = parts[0] if nseg == 1 else jnp.concatenate(parts, axis=0)
        act = (uc * jax.nn.sigmoid(uc) * up).astype(BF16)
        part = _dot(act, wd_ref[c, :])
        down = part if down is None else down + part
    for s in range(nseg):
        rows = slice(s * seg, (s + 1) * seg)
        y_ref[rows, :] = x1_scr[rows, :] + mod_ref[s][:, 5 * d:6 * d] * down[rows, :]


def _post(oa, ob, x2, mod3, goa, gob, gnf, wout, wg, wu, wd, wconv, bconv, cbuf, *, batch, t, tm_max):
    r, d = x2.shape
    ff = wg.shape[1]
    tm, seg, nseg = _row_tiling(batch, t, tm_max)
    per_batch = max(1, t // tm)
    full = lambda shape: pl.BlockSpec(shape, lambda i: (0,) * len(shape))
    rows = lambda w: pl.BlockSpec((tm, w), lambda i: (i, 0))
    if nseg == 1:
        state_spec = pl.BlockSpec((1, CONV_W - 1, ff), lambda i: (i // per_batch, 0, 0))
    else:
        state_spec = pl.BlockSpec((nseg, CONV_W - 1, ff), lambda i: (i, 0, 0))
    return pl.pallas_call(
        functools.partial(_post_kernel, seg=seg, nseg=nseg, d=d, per_batch=per_batch),
        grid=(r // tm,),
        in_specs=[rows(oa.shape[1]), rows(ob.shape[1]), rows(d), _mod_spec(t, tm, nseg, 6 * d),
                  full((1, oa.shape[1])), full((1, ob.shape[1])), full((1, d)),
                  full(wout.shape), full(wg.shape), full(wu.shape), full(wd.shape),
                  full((CONV_W, ff)), full((1, ff)), state_spec],
        out_specs=[rows(d), state_spec],
        out_shape=(jax.ShapeDtypeStruct((r, d), F32),
                   jax.ShapeDtypeStruct((batch, CONV_W - 1, ff), F32)),
        scratch_shapes=[pltpu.VMEM((tm, d), F32), pltpu.VMEM((tm, d), BF16),
                        pltpu.VMEM((nseg, seg + 8, ff), F32)],
        compiler_params=_params("arbitrary"),
        name="post_ffn",
    )(oa, ob, x2, mod3, goa, gob, gnf, wout, wg, wu, wd, wconv, bconv, cbuf)


def _in_cols():
    qa = H_MLA * (D_NOPE + D_ROPE)
    half = D_ROPE // 2
    cols = []
    for h in range(H_MLA):
        base = h * (D_NOPE + D_ROPE)
        pe = base + D_NOPE
        cols += list(range(base, pe + D_ROPE))
        cols += list(range(pe + half, pe + D_ROPE)) + list(range(pe, pe + half))
    cols += list(range(qa, qa + D_LATENT))
    kp = qa + D_LATENT
    cols += list(range(kp, kp + D_ROPE)) + list(range(kp + half, kp + D_ROPE)) + list(range(kp, kp + half))
    cols += list(range(kp + D_ROPE, kp + D_ROPE + 3 * H_SB * D_SB))
    return jnp.asarray(cols, jnp.int32)


def _rope_gain(g):
    half = D_ROPE // 2
    return jnp.concatenate([g, g[..., half:], g[..., :half]], axis=-1)


def _rope_tables(pos):
    half = D_ROPE // 2
    freqs = ROPE_THETA ** (-jnp.arange(half, dtype=F32) / half)
    ang = pos.astype(F32)[:, None] * freqs[None, :]
    cos, sin = jnp.cos(ang), jnp.sin(ang)
    return jnp.tile(cos, (1, 4)), jnp.concatenate([-sin, sin, sin, -sin], axis=-1)


def kernel(x_prompt, x_sample, c_prompt, c_sample, cache_mla_latent, cache_mla_krope, cache_sb_k, cache_sb_v, state_ffn_conv, w_ada, b_ada, g_norm_mix, g_norm_ffn, w_in, g_kv_latent, g_q_nope, g_q_rope, g_k_nope, g_k_rope, w_uk, w_uv, g_out_mla, g_out_sb, w_out, w_gate, w_up, w_conv, b_conv, w_down):
    depth = w_in.shape[0]
    bp, tp, d = x_prompt.shape
    bs, ts, _ = x_sample.shape
    past = cache_mla_latent.shape[2]
    ff = w_gate.shape[2]
    sb = H_SB * D_SB

    nb = bp + bs
    nb_pad = -(-nb // 8) * 8
    c_all = jnp.concatenate([c_prompt, c_sample, jnp.zeros((nb_pad - nb, d), F32)], axis=0)
    mod = _ada_all(c_all, w_ada, b_ada)

    w_in_p = jnp.take(w_in, _in_cols(), axis=2).astype(BF16)
    w_uk16, w_uv16, w_out16 = w_uk.astype(BF16), w_uv.astype(BF16), w_out.astype(BF16)
    w_uvt16 = jnp.swapaxes(w_uv16, 1, 2)
    w_gate16, w_up16, w_down16 = w_gate.astype(BF16), w_up.astype(BF16), w_down.astype(BF16)
    gqr, gkr = _rope_gain(g_q_rope), _rope_gain(g_k_rope)

    cos_p, sin_p = _rope_tables(jnp.arange(tp, dtype=jnp.int32))
    cos_p, sin_p = jnp.tile(cos_p, (bp, 1)), jnp.tile(sin_p, (bp, 1))
    cos_s, sin_s = _rope_tables(past + jnp.arange(ts, dtype=jnp.int32))
    cos_s, sin_s = jnp.tile(cos_s, (bs, 1)), jnp.tile(sin_s, (bs, 1))

    lat_cache_rows = cache_mla_latent.reshape(-1, D_LATENT)

    def layer(l, x2, mod3, cos, sin, batch, t, p):
        row = lambda a: a[l][None, :]
        qm, lat, kpe, kpe128, qb, kb, vb, kb16, vb16 = _inproj(
            x2, mod3, row(g_norm_mix), w_in_p[l], row(g_q_nope), row(gqr), row(g_kv_latent), row(gkr),
            cos, sin, batch=batch, t=t, tm_max=512)
        if p:
            cbuf = state_ffn_conv[l]
            knp, vp = _kvup(lat_cache_rows, w_uk16[l], w_uv16[l], row(g_k_nope), tm_max=512,
                            row0=l * batch * p, n_rows=batch * p)
            knn, vn = _kvup(lat, w_uk16[l], w_uv16[l], row(g_k_nope), tm_max=512)
            oa = _mla_attn_cached(qm, knp, cache_mla_krope, vp, knn, kpe128, vn,
                                  layer=l, batch=batch, t=t, past=p, tk=256)
            ob = _sb_attn(qb, kb16, vb16, batch=batch, t=t, tq=t, tk=256, past=p,
                          k_cache=cache_sb_k, v_cache=cache_sb_v, layer=l)
        else:
            cbuf = jnp.zeros((batch, CONV_W - 1, ff), F32)
            kn, vt = _kvup(lat, w_uk16[l], w_uvt16[l], row(g_k_nope), tm_max=512, per_batch_t=t)
            oa = _mla_attn_first(qm, kn, kpe128, vt, batch=batch, t=t, tq=512, tk=256)
            ob = _sb_attn(qb, kb16, vb16, batch=batch, t=t, tq=256, tk=256)
        y, nconv = _post(oa, ob, x2, mod3, row(g_out_mla), row(g_out_sb), row(g_norm_ffn), w_out16[l],
                         w_gate16[l], w_up16[l], w_down16[l], w_conv[l], row(b_conv), cbuf,
                         batch=batch, t=t, tm_max=256)
        return (y, lat.reshape(batch, t, D_LATENT), kpe.reshape(batch, t, D_ROPE),
                kb.reshape(batch, t, H_SB, D_SB), vb.reshape(batch, t, H_SB, D_SB), nconv)

    xp = x_prompt.reshape(bp * tp, d)
    xs = x_sample.reshape(bs * ts, d)
    outs_p, outs_s = [], []
    for l in range(depth):
        mod_p = mod[l, :bp].reshape(bp, 1, 6 * d)
        mod_s = mod[l, bp:nb].reshape(bs, 1, 6 * d)
        rp = layer(l, xp, mod_p, cos_p, sin_p, bp, tp, 0)
        rs = layer(l, xs, mod_s, cos_s, sin_s, bs, ts, past)
        xp, xs = rp[0], rs[0]
        outs_p.append(rp[1:])
        outs_s.append(rs[1:])

    stack = lambda outs, k: jnp.stack([o[k] for o in outs])
    return ((xp.reshape(bp, tp, d), xs.reshape(bs, ts, d))
            + tuple(stack(outs_p, k) for k in range(5))
            + tuple(stack(outs_s, k) for k in range(5)))
```

```python
import functools
import math

import jax
import jax.numpy as jnp
from jax import lax
from jax.experimental import pallas as pl
from jax.experimental.pallas import tpu as pltpu

F32 = jnp.float32
BF16 = jnp.bfloat16

H_MLA = 4
D_NOPE = 128
D_ROPE = 64
D_VA = 128
D_LATENT = 512
H_SB = 4
D_SB = 128
CONV_W = 3
CHUNK = 64
ROPE_THETA = 10000.0
EPS = 1e-6

LANES = 128
MXU_COLS = 256
Q_HEAD_COLS = 2 * LANES
MASK_VALUE = -1e30
FF_CHUNKS = 2
ONES_ROWS = 16
VT_ROWS = D_VA + ONES_ROWS
SB_EXP_ZERO = -105.0
VMEM_LIMIT_BYTES = 56 * 1024 * 1024

MLA_Q_SCALE = math.log2(math.e) * (D_NOPE + D_ROPE) ** -0.5
SB_SCALE = D_SB ** -0.5


def _rms_rows(x, g):
    return x * lax.rsqrt(jnp.mean(x * x, axis=-1, keepdims=True) + EPS) * g


def _dot(a, b):
    return jnp.dot(a, b, preferred_element_type=F32)


def _dot_nt(a, b):
    return lax.dot_general(a, b, (((1,), (1,)), ((), ())), preferred_element_type=F32)


def _dot_tn(a, b):
    return lax.dot_general(a, b, (((0,), (0,)), ((), ())), preferred_element_type=F32)


def _params(*sem):
    return pltpu.CompilerParams(dimension_semantics=sem, vmem_limit_bytes=VMEM_LIMIT_BYTES)


def _cast_kernel(x_ref, o_ref):
    o_ref[...] = x_ref[...].astype(o_ref.dtype)


def _to_bf16(w):
    x = w.reshape(-1, w.shape[-1])
    r, c = x.shape
    tm = 512
    while r % tm:
        tm //= 2
    out = pl.pallas_call(
        _cast_kernel,
        grid=(r // tm,),
        in_specs=[pl.BlockSpec((tm, c), lambda i: (i, 0))],
        out_specs=pl.BlockSpec((tm, c), lambda i: (i, 0)),
        out_shape=jax.ShapeDtypeStruct((r, c), BF16),
        compiler_params=_params("parallel"),
        name="cast_bf16",
    )(x)
    return out.reshape(w.shape)


def _ada_kernel(c_ref, w_ref, b_ref, o_ref):
    c = c_ref[...]
    s = (c * jax.nn.sigmoid(c)).astype(BF16)
    o_ref[0] = _dot(s, w_ref[0].astype(BF16)) + b_ref[0]


def _ada_all(c_all, w_ada, b_ada):
    depth, d, n = w_ada.shape
    bp = c_all.shape[0]
    tn = d
    return pl.pallas_call(
        _ada_kernel,
        grid=(depth, n // tn),
        in_specs=[
            pl.BlockSpec((bp, d), lambda l, j: (0, 0)),
            pl.BlockSpec((1, d, tn), lambda l, j: (l, 0, j)),
            pl.BlockSpec((1, 1, tn), lambda l, j: (l, 0, j)),
        ],
        out_specs=pl.BlockSpec((1, bp, tn), lambda l, j: (l, 0, j)),
        out_shape=jax.ShapeDtypeStruct((depth, bp, n), F32),
        compiler_params=_params("parallel", "parallel"),
        name="ada_mod",
    )(c_all, w_ada, b_ada.reshape(depth, 1, n))


C_QM = 0
C_LAT = H_MLA * Q_HEAD_COLS
C_KPE = C_LAT + D_LATENT
C_QB = C_KPE + LANES
C_KB = C_QB + H_SB * D_SB
C_VB = C_KB + H_SB * D_SB
C_END = C_VB + H_SB * D_SB


def _rope_pair(v, g, cos, sin):
    ssq = jnp.sum(v * v, axis=-1, keepdims=True) * (0.5 / D_ROPE)
    vn = v * lax.rsqrt(ssq + EPS) * g
    return vn * cos + pltpu.roll(vn, D_ROPE, 1) * sin


N_INPROJ_IN = 10


def _inproj_kernel(*refs, seg, nseg, d):
    x_ref, mod_ref, gn_ref, w_ref, gqn_ref, gqr_ref, glat_ref, gkr_ref, cos_ref, sin_ref = refs[:N_INPROJ_IN]
    qm_ref, lat_ref, kpe_ref, kpe128_ref, qb_ref, kb_ref, vb_ref, kb16_ref, vb16_ref, h_scr = refs[-10:]
    for s in range(nseg):
        rows = slice(s * seg, (s + 1) * seg)
        m = mod_ref[s]
        h = _rms_rows(x_ref[rows, :], gn_ref[...]) * (1.0 + m[:, d:2 * d]) + m[:, 0:d]
        h_scr[rows, :] = h.astype(BF16)
    h = h_scr[...]
    cos = cos_ref[...]
    sin = sin_ref[...]

    for hd in range(H_MLA):
        c0 = C_QM + hd * Q_HEAD_COLS
        qh = _dot(h, w_ref[:, c0:c0 + Q_HEAD_COLS])
        qn = _rms_rows(qh[:, :D_NOPE], gqn_ref[...]) * MLA_Q_SCALE
        qp = _rope_pair(qh[:, D_NOPE:], gqr_ref[...], cos, sin) * MLA_Q_SCALE
        qm_ref[:, c0:c0 + D_NOPE] = qn.astype(BF16)
        qm_ref[:, c0 + D_NOPE:c0 + Q_HEAD_COLS] = qp.astype(BF16)

    lat_ref[...] = _rms_rows(_dot(h, w_ref[:, C_LAT:C_KPE]), glat_ref[...])

    kp = _rope_pair(_dot(h, w_ref[:, C_KPE:C_QB]), gkr_ref[...], cos, sin)
    kpe_ref[...] = kp[:, :D_ROPE]
    lane = lax.broadcasted_iota(jnp.int32, kp.shape, 1)
    kpe128_ref[...] = jnp.where(lane < D_ROPE, kp, 0.0).astype(BF16)

    qb_ref[...] = (_dot(h, w_ref[:, C_QB:C_KB]) * SB_SCALE).astype(BF16)
    kb = _dot(h, w_ref[:, C_KB:C_VB])
    kb_ref[...] = kb
    kb16_ref[...] = kb.astype(BF16)
    vb = _dot(h, w_ref[:, C_VB:C_END])
    vb_ref[...] = vb
    vb16_ref[...] = vb.astype(BF16)


def _row_tiling(batch, t, tm_max):
    if t >= tm_max:
        assert t % tm_max == 0
        return tm_max, tm_max, 1
    nseg = max(1, min(batch, tm_max // t))
    while batch % nseg:
        nseg -= 1
    return nseg * t, t, nseg


def _mod_spec(t, tm, nseg, width):
    if nseg == 1:
        per = t // tm
        return pl.BlockSpec((1, 1, width), lambda i: (i // per, 0, 0))
    return pl.BlockSpec((nseg, 1, width), lambda i: (i, 0, 0))


STACKED_OUTS = (1, 2, 5, 6)


def _inproj(x2, mod3, gn, w_p, gqn, gqr, glat, gkr, cos, sin, *, batch, t, tm_max, layer, depth, stacks):
    r, d = x2.shape
    tm, seg, nseg = _row_tiling(batch, t, tm_max)
    full = lambda shape: pl.BlockSpec(shape, lambda i: (0,) * len(shape))
    rows = lambda w: pl.BlockSpec((tm, w), lambda i: (i, 0))
    sb = H_SB * D_SB
    widths = ((H_MLA * Q_HEAD_COLS, BF16),
              (D_LATENT, F32),
              (D_ROPE, F32),
              (LANES, BF16),
              (sb, BF16),
              (sb, F32), (sb, F32),
              (sb, BF16), (sb, BF16))
    out_shapes, out_specs = [], []
    for k, (w, dt) in enumerate(widths):
        if k in STACKED_OUTS:
            out_shapes.append(jax.ShapeDtypeStruct((depth, r, w), dt))
            out_specs.append(pl.BlockSpec((None, tm, w), lambda i: (layer, i, 0)))
        else:
            out_shapes.append(jax.ShapeDtypeStruct((r, w), dt))
            out_specs.append(rows(w))
    in_specs = [rows(d), _mod_spec(t, tm, nseg, 6 * d), full((1, d)), full(w_p.shape),
                full((1, D_NOPE)), full((1, LANES)), full((1, D_LATENT)), full((1, LANES)),
                rows(LANES), rows(LANES)]
    operands = (x2, mod3, gn, w_p, gqn, gqr, glat, gkr, cos, sin)
    assert len(operands) == N_INPROJ_IN
    aliases = {}
    if stacks is not None:
        in_specs += [pl.BlockSpec(memory_space=pl.ANY)] * len(STACKED_OUTS)
        operands += tuple(stacks)
        aliases = {N_INPROJ_IN + n: k for n, k in enumerate(STACKED_OUTS)}
    return pl.pallas_call(
        functools.partial(_inproj_kernel, seg=seg, nseg=nseg, d=d),
        grid=(r // tm,),
        in_specs=in_specs,
        out_specs=out_specs,
        out_shape=tuple(out_shapes),
        input_output_aliases=aliases,
        scratch_shapes=[pltpu.VMEM((tm, d), BF16)],
        compiler_params=_params("parallel"),
        name="in_proj",
    )(*operands)


def _kvup_kernel(lat_ref, wuk_ref, wuv_ref, gkn_ref, k_ref, v_ref, *, v_transposed):
    l16 = lat_ref[...].astype(BF16)
    kn = _dot(l16, wuk_ref[...])
    for hd in range(H_MLA):
        cols = slice(hd * D_NOPE, (hd + 1) * D_NOPE)
        k_ref[:, cols] = _rms_rows(kn[:, cols], gkn_ref[...]).astype(BF16)
    if v_transposed:
        vt = _dot_nt(wuv_ref[...], l16).astype(BF16)
        ones = jnp.ones((ONES_ROWS, vt.shape[1]), BF16)
        for hd in range(H_MLA):
            v_ref[hd * VT_ROWS:hd * VT_ROWS + D_VA, :] = vt[hd * D_VA:(hd + 1) * D_VA, :]
            v_ref[hd * VT_ROWS + D_VA:(hd + 1) * VT_ROWS, :] = ones
    else:
        v_ref[...] = _dot(l16, wuv_ref[...]).astype(BF16)


def _kvup(lat_all, wuk, wuv, gkn, *, tm_max, per_batch_t=None, row0=0, n_rows=None):
    r = lat_all.shape[0] if n_rows is None else n_rows
    tm = min(tm_max, r if per_batch_t is None else per_batch_t)
    while r % tm or row0 % tm:
        tm //= 2
    off = row0 // tm
    full = lambda shape: pl.BlockSpec(shape, lambda i: (0,) * len(shape))
    rows = lambda w: pl.BlockSpec((tm, w), lambda i: (i, 0))
    vcols = H_MLA * D_VA
    if per_batch_t is None:
        v_spec, v_shape = rows(vcols), jax.ShapeDtypeStruct((r, vcols), BF16)
    else:
        per = per_batch_t // tm
        v_spec = pl.BlockSpec((None, H_MLA * VT_ROWS, tm), lambda i: (i // per, 0, i % per))
        v_shape = jax.ShapeDtypeStruct((r // per_batch_t, H_MLA * VT_ROWS, per_batch_t), BF16)
    return pl.pallas_call(
        functools.partial(_kvup_kernel, v_transposed=per_batch_t is not None),
        grid=(r // tm,),
        in_specs=[pl.BlockSpec((tm, D_LATENT), lambda i: (i + off, 0)),
                  full(wuk.shape), full(wuv.shape), full((1, D_NOPE))],
        out_specs=[rows(H_MLA * D_NOPE), v_spec],
        out_shape=(jax.ShapeDtypeStruct((r, H_MLA * D_NOPE), BF16), v_shape),
        compiler_params=_params("parallel"),
        name="kv_up",
    )(lat_all, wuk, wuv, gkn)


def _mla_cached_kernel(q_ref, knp_ref, kpep_ref, vp_ref, knn_ref, kpen_ref, vn_ref, o_ref,
                       m_scr, l_scr, acc_scr, *, t, tk, past):
    m_scr[...] = jnp.full(m_scr.shape, MASK_VALUE, F32)
    l_scr[...] = jnp.zeros(l_scr.shape, F32)
    acc_scr[...] = jnp.zeros(acc_scr.shape, F32)
    qk = D_NOPE + D_ROPE

    def update(kn, kpe, v, mask):
        ss = []
        for hd in range(H_MLA):
            kb = jnp.concatenate([kn[:, hd * D_NOPE:(hd + 1) * D_NOPE], kpe], axis=-1)
            ss.append(_dot_nt(kb, q_ref[:, hd * Q_HEAD_COLS:hd * Q_HEAD_COLS + qk]))
        for hd, s in enumerate(ss):
            if mask is not None:
                s = jnp.where(mask, s, MASK_VALUE)
            m_old = m_scr[hd]
            m_new = jnp.maximum(m_old, jnp.max(s, axis=0, keepdims=True))
            alpha = jnp.exp2(m_old - m_new)
            p = jnp.exp2(s - m_new)
            l_scr[hd] = alpha * l_scr[hd] + jnp.sum(p, axis=0, keepdims=True)
            acc_scr[hd] = alpha * acc_scr[hd] + _dot_tn(v[:, hd * D_VA:(hd + 1) * D_VA], p.astype(BF16))
            m_scr[hd] = m_new

    def body(j, carry):
        rows = pl.ds(pl.multiple_of(j * tk, tk), tk)
        update(knp_ref[rows, :], kpep_ref[rows, :].astype(BF16), vp_ref[rows, :], None)
        return carry

    lax.fori_loop(0, past // tk, body, 0)

    if t > CHUNK:
        kc = lax.broadcasted_iota(jnp.int32, (t, t), 0) // CHUNK
        qc = lax.broadcasted_iota(jnp.int32, (t, t), 1) // CHUNK
        mask = kc <= qc
    else:
        mask = None
    update(knn_ref[...], kpen_ref[:, 0:D_ROPE], vn_ref[...], mask)

    for hd in range(H_MLA):
        o = acc_scr[hd] * (1.0 / l_scr[hd])
        o_ref[:, hd * D_VA:(hd + 1) * D_VA] = o.T


def _mla_attn_cached(q, knp, kpe_cache, vp, knn, kpen, vn, *, layer, batch, t, past, tk):
    assert past % tk == 0 and past % CHUNK == 0 and t % CHUNK == 0
    vcols = H_MLA * D_VA
    new = lambda a: pl.BlockSpec((t, a.shape[1]), lambda b: (b, 0))
    old = lambda a: pl.BlockSpec((past, a.shape[1]), lambda b: (b, 0))
    scratch = [pltpu.VMEM((H_MLA, 1, t), F32), pltpu.VMEM((H_MLA, 1, t), F32), pltpu.VMEM((H_MLA, D_VA, t), F32)]
    return pl.pallas_call(
        functools.partial(_mla_cached_kernel, t=t, tk=tk, past=past),
        grid=(batch,),
        in_specs=[new(q), old(knp),
                  pl.BlockSpec((None, None, past, D_ROPE), lambda b: (layer, b, 0, 0)),
                  old(vp), new(knn), new(kpen), new(vn)],
        out_specs=pl.BlockSpec((t, vcols), lambda b: (b, 0)),
        out_shape=jax.ShapeDtypeStruct((batch * t, vcols), F32),
        scratch_shapes=scratch,
        compiler_params=_params("parallel"),
        name="mla_attn_cached",
    )(q, knp, kpe_cache, vp, knn, kpen, vn)


def _mla_first_kernel(q_ref, kn_ref, kpe_ref, vt_ref, o_ref, qt_scr, m_scr, acc_scr, s_scr, p_scr, a_scr,
                      *, tq, tk):
    i = pl.program_id(1)
    nb = tq // tk
    assert nb == 2
    for hd in range(H_MLA):
        qt_scr[hd] = q_ref[:, hd * Q_HEAD_COLS:(hd + 1) * Q_HEAD_COLS].T
    m_scr[...] = jnp.full(m_scr.shape, MASK_VALUE, F32)
    acc_scr[...] = jnp.zeros(acc_scr.shape, F32)
    last = i * nb + nb - 1

    def stage_a(b, buf):
        r0 = pl.multiple_of(b * tk, tk)
        kpe = kpe_ref[pl.ds(r0, tk), :]
        for hd in range(H_MLA):
            kb = jnp.concatenate([kn_ref[pl.ds(r0, tk), hd * D_NOPE:(hd + 1) * D_NOPE], kpe], axis=-1)
            s_scr[buf, hd] = _dot(kb, qt_scr[hd])

    def stage_b(buf, own):
        for hd in range(H_MLA):
            for c0 in range(0, tq, LANES):
                qs = slice(c0, c0 + LANES)
                s = s_scr[buf, hd, :, qs]
                if own is not None:
                    kc = lax.broadcasted_iota(jnp.int32, (tk, LANES), 0) // CHUNK + own * (tk // CHUNK)
                    qc = (lax.broadcasted_iota(jnp.int32, (tk, LANES), 1) + c0) // CHUNK
                    s = jnp.where(kc <= qc, s, MASK_VALUE)
                m_old = m_scr[hd, :, qs]
                m_new = jnp.maximum(m_old, jnp.max(s, axis=0, keepdims=True))
                p = jnp.exp2(s - m_new)
                m_scr[hd, :, qs] = m_new
                a_scr[buf, hd, :, qs] = jnp.exp2(m_old - m_new)
                p_scr[buf, hd, :, qs] = p.astype(BF16)

    def stage_c(b, buf):
        r0 = pl.multiple_of(b * tk, tk)
        for hd in range(H_MLA):
            vt = vt_ref[hd * VT_ROWS:(hd + 1) * VT_ROWS, pl.ds(r0, tk)]
            acc_scr[hd] = a_scr[buf, hd] * acc_scr[hd] + _dot(vt, p_scr[buf, hd])

    def step(t, buf, own=None):
        stage_a(t, buf)
        stage_c(t - 2, buf)
        stage_b(1 - buf, own)

    @pl.when(i == 0)
    def _():
        stage_a(0, 0)
        stage_a(1, 1)
        stage_b(0, 0)
        stage_c(0, 0)
        stage_b(1, 1)
        stage_c(1, 1)

    @pl.when(i > 0)
    def _():
        stage_a(0, 0)
        stage_a(1, 1)
        stage_b(0, None)
        step(2, 0)

        pairs = i - 1

        def quad(k, carry):
            t = 3 + 4 * k
            step(t, 1)
            step(t + 1, 0)
            step(t + 2, 1)
            step(t + 3, 0)
            return carry

        lax.fori_loop(0, pairs // 2, quad, 0)

        def pair(k, carry):
            t = 3 + 2 * k
            step(t, 1)
            step(t + 1, 0)
            return carry

        lax.fori_loop(pairs // 2 * 2, pairs, pair, 0)
        step(last, 1, own=0)
        stage_c(last - 1, 0)
        stage_b(1, 1)
        stage_c(last, 1)

    for hd in range(H_MLA):
        o = acc_scr[hd, 0:D_VA, :] * (1.0 / acc_scr[hd, D_VA:D_VA + 1, :])
        o_ref[:, hd * D_VA:(hd + 1) * D_VA] = o.T


def _mla_attn_first(q, kn, kpe, vt, *, batch, t, tq, tk):
    assert t % tq == 0 and tq == 2 * tk and tk % CHUNK == 0
    nq = t // tq
    vcols = H_MLA * D_VA
    resident = lambda a: pl.BlockSpec((t, a.shape[1]), lambda b, i: (b, 0), pipeline_mode=pl.Buffered(1))
    scratch = [pltpu.VMEM((H_MLA, Q_HEAD_COLS, tq), BF16),
               pltpu.VMEM((H_MLA, 1, tq), F32),
               pltpu.VMEM((H_MLA, VT_ROWS, tq), F32),
               pltpu.VMEM((2, H_MLA, tk, tq), F32), pltpu.VMEM((2, H_MLA, tk, tq), BF16),
               pltpu.VMEM((2, H_MLA, 1, tq), F32)]
    return pl.pallas_call(
        functools.partial(_mla_first_kernel, tq=tq, tk=tk),
        grid=(batch, nq),
        in_specs=[pl.BlockSpec((tq, q.shape[1]), lambda b, i: (b * nq + i, 0)), resident(kn), resident(kpe),
                  pl.BlockSpec((None, H_MLA * VT_ROWS, t), lambda b, i: (b, 0, 0), pipeline_mode=pl.Buffered(1))],
        out_specs=pl.BlockSpec((tq, vcols), lambda b, i: (b * nq + i, 0)),
        out_shape=jax.ShapeDtypeStruct((batch * t, vcols), F32),
        scratch_shapes=scratch,
        compiler_params=_params("parallel", "arbitrary"),
        name="mla_attn_first",
    )(q, kn, kpe, vt)


def _sb_kernel(q_ref, *refs, tq, tk, past):
    if past:
        kp_ref, vp_ref, k_ref, v_ref, o_ref, c_scr, acc_scr = refs
    else:
        k_ref, v_ref, o_ref, c_scr, acc_scr = refs
    i = pl.program_id(1)
    c_scr[...] = jnp.zeros(c_scr.shape, F32)
    acc_scr[...] = jnp.zeros(acc_scr.shape, F32)
    heads = [slice(hd * D_SB, (hd + 1) * D_SB) for hd in range(H_SB)]

    def later_eq(n):
        return (lax.broadcasted_iota(jnp.int32, (n, n), 1)
                >= lax.broadcasted_iota(jnp.int32, (n, n), 0)).astype(BF16)

    def new_rows(r0, n):
        return ([k_ref[pl.ds(r0, n), cols] for cols in heads], [v_ref[pl.ds(r0, n), cols] for cols in heads])

    def cached_rows(r0, n):
        return ([kp_ref[pl.ds(r0, n), hd, :].astype(BF16) for hd in range(H_SB)],
                [vp_ref[pl.ds(r0, n), hd, :].astype(BF16) for hd in range(H_SB)])

    def update(kv, n, tri, before):
        ks, vs = kv
        zs = [_dot_nt(ks[hd], q_ref[:, cols]) for hd, cols in enumerate(heads)]
        es, incls = [], []
        for z in zs:
            ls = jnp.minimum(-z, 0.0) - jnp.log(1.0 + jnp.exp(-jnp.abs(z)))
            lsm = ls if before is None else jnp.where(before, ls, 0.0)
            incl = _dot(tri, lsm.astype(BF16))
            incls.append(incl)
            es.append(z + incl if before is None else z + ls + (incl - lsm))
        top = None
        for hd in range(H_SB):
            c = c_scr[hd]
            w = jnp.exp(es[hd] + c)
            if before is not None:
                w = jnp.where(before, w, 0.0)
            acc_scr[hd] += _dot_tn(vs[hd], w.astype(BF16))
            c_new = c + incls[hd][0:1, :]
            c_scr[hd] = c_new
            top_h = jnp.max(c_new)
            top = top_h if top is None else jnp.maximum(top, top_h)
        return top

    own0 = i * tq
    before = lax.broadcasted_iota(jnp.int32, (tq, tq), 0) < lax.broadcasted_iota(jnp.int32, (tq, tq), 1)
    top = update(new_rows(pl.multiple_of(own0, tq), tq), tq, later_eq(tq), before)
    tri_k = later_eq(tk)
    older = cached_rows if past else new_rows

    def cond(st):
        j, top = st
        return jnp.logical_and(j >= 0, top > SB_EXP_ZERO)

    def body(st):
        j, _ = st
        return j - 1, update(older(pl.multiple_of(j * tk, tk), tk), tk, tri_k, None)

    lax.while_loop(cond, body, ((past + own0) // tk - 1, top))
    for hd in range(H_SB):
        o_ref[:, hd * D_SB:(hd + 1) * D_SB] = acc_scr[hd].T


def _sb_attn(q, k, v, *, batch, t, tq, tk, past=0, k_cache=None, v_cache=None, layer=0):
    assert t % tq == 0 and past % tk == 0 and (tq == tk if past == 0 else t == tq)
    nq = t // tq
    cols = H_SB * D_SB
    q_spec = pl.BlockSpec((tq, cols), lambda b, i: (b * nq + i, 0))
    if past:
        cache = pl.BlockSpec((None, None, past, H_SB, D_SB), lambda b, i: (layer, b, 0, 0, 0))
        new = pl.BlockSpec((t, cols), lambda b, i: (b, 0))
        in_specs, operands = [q_spec, cache, cache, new, new], (q, k_cache, v_cache, k, v)
    else:
        resident = pl.BlockSpec((t, cols), lambda b, i: (b, 0), pipeline_mode=pl.Buffered(1))
        in_specs, operands = [q_spec, resident, resident], (q, k, v)
    return pl.pallas_call(
        functools.partial(_sb_kernel, tq=tq, tk=tk, past=past),
        grid=(batch, nq),
        in_specs=in_specs,
        out_specs=q_spec,
        out_shape=jax.ShapeDtypeStruct((batch * t, cols), F32),
        scratch_shapes=[pltpu.VMEM((H_SB, 1, tq), F32), pltpu.VMEM((H_SB, D_SB, tq), F32)],
        compiler_params=_params("parallel", "arbitrary"),
        name="sb_attn",
    )(*operands)


def _post_kernel(oa_ref, ob_ref, x_ref, mod_ref, goa_ref, gob_ref, gnf_ref, wout_ref, wg_ref, wu_ref, wd_ref,
                 wconv_ref, bconv_ref, cbuf_ref, y_ref, nconv_ref, x1_scr, h_scr, u_scr,
                 *, seg, nseg, d, per_batch):
    i = pl.program_id(0)
    na = oa_ref.shape[1]
    ma = _rms_rows(oa_ref[...], goa_ref[...]).astype(BF16)
    mb = _rms_rows(ob_ref[...], gob_ref[...]).astype(BF16)
    mix = _dot(ma, wout_ref[0:na, :]) + _dot(mb, wout_ref[na:, :])
    for s in range(nseg):
        rows = slice(s * seg, (s + 1) * seg)
        m = mod_ref[s]
        x1 = x_ref[rows, :] + m[:, 2 * d:3 * d] * mix[rows, :]
        x1_scr[rows, :] = x1
        h = _rms_rows(x1, gnf_ref[...]) * (1.0 + m[:, 4 * d:5 * d]) + m[:, 3 * d:4 * d]
        h_scr[rows, :] = h.astype(BF16)
    h = h_scr[...]

    if nseg == 1:
        first = i % per_batch == 0

        @pl.when(first)
        def _():
            u_scr[0, 8 - (CONV_W - 1):8, :] = cbuf_ref[0]

        @pl.when(jnp.logical_not(first))
        def _():
            u_scr[0, 0:8, :] = u_scr[0, seg:seg + 8, :]
    else:
        for s in range(nseg):
            u_scr[s, 8 - (CONV_W - 1):8, :] = cbuf_ref[s]

    ff = wg_ref.shape[1]
    bounds = [ff * c // FF_CHUNKS // MXU_COLS * MXU_COLS for c in range(FF_CHUNKS)] + [ff]
    chunks = [slice(bounds[c], bounds[c + 1]) for c in range(FF_CHUNKS)]
    us = [_dot(h, wg_ref[:, c]) for c in chunks]
    ups = [_dot(h, wu_ref[:, c]) for c in chunks]
    down = None
    for c, u, up in zip(chunks, us, ups):
        for s in range(nseg):
            u_scr[s, 8:8 + seg, c] = u[s * seg:(s + 1) * seg, :]
            nconv_ref[s, :, c] = u[(s + 1) * seg - (CONV_W - 1):(s + 1) * seg, :]
        parts = []
        for s in range(nseg):
            uc = bconv_ref[:, c]
            for tap in range(CONV_W):
                lo = 8 - (CONV_W - 1) + tap
                uc = uc + u_scr[s, lo:lo + seg, c] * wconv_ref[tap:tap + 1, c]
            parts.append(uc)
        uc = parts[0] if nseg == 1 else jnp.concatenate(parts, axis=0)
        act = (uc * jax.nn.sigmoid(uc) * up).astype(BF16)
        part = _dot(act, wd_ref[c, :])
        down = part if down is None else down + part
    for s in range(nseg):
        rows = slice(s * seg, (s + 1) * seg)
        y_ref[rows, :] = x1_scr[rows, :] + mod_ref[s][:, 5 * d:6 * d] * down[rows, :]


def _post(oa, ob, x2, mod3, goa, gob, gnf, wout, wg, wu, wd, wconv, bconv, cbuf, *, batch, t, tm_max):
    r, d = x2.shape
    ff = wg.shape[1]
    tm, seg, nseg = _row_tiling(batch, t, tm_max)
    per_batch = max(1, t // tm)
    full = lambda shape: pl.BlockSpec(shape, lambda i: (0,) * len(shape))
    rows = lambda w: pl.BlockSpec((tm, w), lambda i: (i, 0))
    if nseg == 1:
        state_spec = pl.BlockSpec((1, CONV_W - 1, ff), lambda i: (i // per_batch, 0, 0))
    else:
        state_spec = pl.BlockSpec((nseg, CONV_W - 1, ff), lambda i: (i, 0, 0))
    return pl.pallas_call(
        functools.partial(_post_kernel, seg=seg, nseg=nseg, d=d, per_batch=per_batch),
        grid=(r // tm,),
        in_specs=[rows(oa.shape[1]), rows(ob.shape[1]), rows(d), _mod_spec(t, tm, nseg, 6 * d),
                  full((1, oa.shape[1])), full((1, ob.shape[1])), full((1, d)),
                  full(wout.shape), full(wg.shape), full(wu.shape), full(wd.shape),
                  full((CONV_W, ff)), full((1, ff)), state_spec],
        out_specs=[rows(d), state_spec],
        out_shape=(jax.ShapeDtypeStruct((r, d), F32),
                   jax.ShapeDtypeStruct((batch, CONV_W - 1, ff), F32)),
        scratch_shapes=[pltpu.VMEM((tm, d), F32), pltpu.VMEM((tm, d), BF16),
                        pltpu.VMEM((nseg, seg + 8, ff), F32)],
        compiler_params=_params("arbitrary"),
        name="post_ffn",
    )(oa, ob, x2, mod3, goa, gob, gnf, wout, wg, wu, wd, wconv, bconv, cbuf)


def _in_col_runs():
    qa = H_MLA * (D_NOPE + D_ROPE)
    half = D_ROPE // 2
    runs = []
    for h in range(H_MLA):
        base = h * (D_NOPE + D_ROPE)
        pe = base + D_NOPE
        runs += [(base, pe + D_ROPE), (pe + half, pe + D_ROPE), (pe, pe + half)]
    runs.append((qa, qa + D_LATENT))
    kp = qa + D_LATENT
    runs += [(kp, kp + D_ROPE), (kp + half, kp + D_ROPE), (kp, kp + half)]
    runs.append((kp + D_ROPE, kp + D_ROPE + 3 * H_SB * D_SB))
    return runs


def _win_kernel(w_ref, o_ref):
    w = w_ref[...]
    o_ref[...] = jnp.concatenate([w[:, a:b] for a, b in _in_col_runs()], axis=1).astype(o_ref.dtype)


def _permute_w_in(w_in):
    depth, d, n = w_in.shape
    x = w_in.reshape(depth * d, n)
    tm = 256
    out = pl.pallas_call(
        _win_kernel,
        grid=(depth * d // tm,),
        in_specs=[pl.BlockSpec((tm, n), lambda i: (i, 0))],
        out_specs=pl.BlockSpec((tm, C_END), lambda i: (i, 0)),
        out_shape=jax.ShapeDtypeStruct((depth * d, C_END), BF16),
        compiler_params=_params("parallel"),
        name="permute_w_in",
    )(x)
    return out.reshape(depth, d, C_END)


def _rope_gain(g):
    half = D_ROPE // 2
    return jnp.concatenate([g, g[..., half:], g[..., :half]], axis=-1)


def _rope_tables(pos):
    half = D_ROPE // 2
    freqs = ROPE_THETA ** (-jnp.arange(half, dtype=F32) / half)
    ang = pos.astype(F32)[:, None] * freqs[None, :]
    cos, sin = jnp.cos(ang), jnp.sin(ang)
    return jnp.tile(cos, (1, 4)), jnp.concatenate([-sin, sin, sin, -sin], axis=-1)


def kernel(x_prompt, x_sample, c_prompt, c_sample, cache_mla_latent, cache_mla_krope, cache_sb_k, cache_sb_v, state_ffn_conv, w_ada, b_ada, g_norm_mix, g_norm_ffn, w_in, g_kv_latent, g_q_nope, g_q_rope, g_k_nope, g_k_rope, w_uk, w_uv, g_out_mla, g_out_sb, w_out, w_gate, w_up, w_conv, b_conv, w_down):
    depth = w_in.shape[0]
    bp, tp, d = x_prompt.shape
    bs, ts, _ = x_sample.shape
    past = cache_mla_latent.shape[2]
    ff = w_gate.shape[2]
    sb = H_SB * D_SB

    nb = bp + bs
    nb_pad = -(-nb // 8) * 8
    c_all = jnp.concatenate([c_prompt, c_sample, jnp.zeros((nb_pad - nb, d), F32)], axis=0)
    mod = _ada_all(c_all, w_ada, b_ada)

    w_in_p = _permute_w_in(w_in)
    w_uk16, w_uv16, w_out16 = _to_bf16(w_uk), _to_bf16(w_uv), _to_bf16(w_out)
    w_uvt16 = jnp.swapaxes(w_uv16, 1, 2)
    w_gate16, w_up16, w_down16 = _to_bf16(w_gate), _to_bf16(w_up), _to_bf16(w_down)
    gqr, gkr = _rope_gain(g_q_rope), _rope_gain(g_k_rope)

    cos_p, sin_p = _rope_tables(jnp.arange(tp, dtype=jnp.int32))
    cos_p, sin_p = jnp.tile(cos_p, (bp, 1)), jnp.tile(sin_p, (bp, 1))
    cos_s, sin_s = _rope_tables(past + jnp.arange(ts, dtype=jnp.int32))
    cos_s, sin_s = jnp.tile(cos_s, (bs, 1)), jnp.tile(sin_s, (bs, 1))

    lat_cache_rows = cache_mla_latent.reshape(-1, D_LATENT)

    def layer(l, x2, mod3, cos, sin, batch, t, p, stacks):
        row = lambda a: a[l][None, :]
        qm, lat_st, kpe_st, kpe128, qb, kb_st, vb_st, kb16, vb16 = _inproj(
            x2, mod3, row(g_norm_mix), w_in_p[l], row(g_q_nope), row(gqr), row(g_kv_latent), row(gkr),
            cos, sin, batch=batch, t=t, tm_max=512, layer=l, depth=depth, stacks=stacks)
        new_lat = dict(row0=l * batch * t, n_rows=batch * t)
        lat_rows = lat_st.reshape(-1, D_LATENT)
        if p:
            cbuf = state_ffn_conv[l]
            knp, vp = _kvup(lat_cache_rows, w_uk16[l], w_uv16[l], row(g_k_nope), tm_max=512,
                            row0=l * batch * p, n_rows=batch * p)
            knn, vn = _kvup(lat_rows, w_uk16[l], w_uv16[l], row(g_k_nope), tm_max=512, **new_lat)
            oa = _mla_attn_cached(qm, knp, cache_mla_krope, vp, knn, kpe128, vn,
                                  layer=l, batch=batch, t=t, past=p, tk=256)
            ob = _sb_attn(qb, kb16, vb16, batch=batch, t=t, tq=t, tk=256, past=p,
                          k_cache=cache_sb_k, v_cache=cache_sb_v, layer=l)
        else:
            cbuf = jnp.zeros((batch, CONV_W - 1, ff), F32)
            kn, vt = _kvup(lat_rows, w_uk16[l], w_uvt16[l], row(g_k_nope), tm_max=512, per_batch_t=t, **new_lat)
            oa = _mla_attn_first(qm, kn, kpe128, vt, batch=batch, t=t, tq=512, tk=256)
            ob = _sb_attn(qb, kb16, vb16, batch=batch, t=t, tq=256, tk=256)
        y, nconv = _post(oa, ob, x2, mod3, row(g_out_mla), row(g_out_sb), row(g_norm_ffn), w_out16[l],
                         w_gate16[l], w_up16[l], w_down16[l], w_conv[l], row(b_conv), cbuf,
                         batch=batch, t=t, tm_max=256)
        return y, (lat_st, kpe_st, kb_st, vb_st), nconv

    xp = x_prompt.reshape(bp * tp, d)
    xs = x_sample.reshape(bs * ts, d)
    st_p = st_s = None
    conv_p, conv_s = [], []
    for l in range(depth):
        mod_p = mod[l, :bp].reshape(bp, 1, 6 * d)
        mod_s = mod[l, bp:nb].reshape(bs, 1, 6 * d)
        xp, st_p, nc_p = layer(l, xp, mod_p, cos_p, sin_p, bp, tp, 0, st_p)
        xs, st_s, nc_s = layer(l, xs, mod_s, cos_s, sin_s, bs, ts, past, st_s)
        conv_p.append(nc_p)
        conv_s.append(nc_s)

    def step_outputs(stacks, convs, batch, t):
        lat_st, kpe_st, kb_st, vb_st = stacks
        return (lat_st.reshape(depth, batch, t, D_LATENT), kpe_st.reshape(depth, batch, t, D_ROPE),
                kb_st.reshape(depth, batch, t, H_SB, D_SB), vb_st.reshape(depth, batch, t, H_SB, D_SB),
                jnp.stack(convs))

    return ((xp.reshape(bp, tp, d), xs.reshape(bs, ts, d))
            + step_outputs(st_p, conv_p, bp, tp) + step_outputs(st_s, conv_s, bs, ts))
```

```python
import functools
import math

import jax
import jax.numpy as jnp
from jax import lax
from jax.experimental import pallas as pl
from jax.experimental.pallas import tpu as pltpu

F32 = jnp.float32
BF16 = jnp.bfloat16

H_MLA = 4
D_NOPE = 128
D_ROPE = 64
D_VA = 128
D_LATENT = 512
H_SB = 4
D_SB = 128
CONV_W = 3
CHUNK = 64
ROPE_THETA = 10000.0
EPS = 1e-6

LANES = 128
MXU_COLS = 256
Q_HEAD_COLS = 2 * LANES
MASK_VALUE = -1e30
FF_CHUNKS = 2
ONES_ROWS = 16
VT_ROWS = D_VA + ONES_ROWS
SB_EXP_ZERO = -105.0
VMEM_LIMIT_BYTES = 56 * 1024 * 1024

MLA_Q_SCALE = math.log2(math.e) * (D_NOPE + D_ROPE) ** -0.5
SB_SCALE = D_SB ** -0.5


def _rms_rows(x, g):
    return x * lax.rsqrt(jnp.mean(x * x, axis=-1, keepdims=True) + EPS) * g


def _dot(a, b):
    return jnp.dot(a, b, preferred_element_type=F32)


def _dot_nt(a, b):
    return lax.dot_general(a, b, (((1,), (1,)), ((), ())), preferred_element_type=F32)


def _dot_tn(a, b):
    return lax.dot_general(a, b, (((0,), (0,)), ((), ())), preferred_element_type=F32)


def _params(*sem):
    return pltpu.CompilerParams(dimension_semantics=sem, vmem_limit_bytes=VMEM_LIMIT_BYTES)


def _cast_kernel(x_ref, o_ref):
    o_ref[...] = x_ref[...].astype(o_ref.dtype)


def _to_bf16(w):
    x = w.reshape(-1, w.shape[-1])
    r, c = x.shape
    tm = 512
    while r % tm:
        tm //= 2
    out = pl.pallas_call(
        _cast_kernel,
        grid=(r // tm,),
        in_specs=[pl.BlockSpec((tm, c), lambda i: (i, 0))],
        out_specs=pl.BlockSpec((tm, c), lambda i: (i, 0)),
        out_shape=jax.ShapeDtypeStruct((r, c), BF16),
        compiler_params=_params("parallel"),
        name="cast_bf16",
    )(x)
    return out.reshape(w.shape)


def _ada_kernel(c_ref, w_ref, b_ref, o_ref):
    c = c_ref[...]
    s = (c * jax.nn.sigmoid(c)).astype(BF16)
    o_ref[0] = _dot(s, w_ref[0].astype(BF16)) + b_ref[0]


def _ada_all(c_all, w_ada, b_ada):
    depth, d, n = w_ada.shape
    bp = c_all.shape[0]
    tn = d
    return pl.pallas_call(
        _ada_kernel,
        grid=(depth, n // tn),
        in_specs=[
            pl.BlockSpec((bp, d), lambda l, j: (0, 0)),
            pl.BlockSpec((1, d, tn), lambda l, j: (l, 0, j)),
            pl.BlockSpec((1, 1, tn), lambda l, j: (l, 0, j)),
        ],
        out_specs=pl.BlockSpec((1, bp, tn), lambda l, j: (l, 0, j)),
        out_shape=jax.ShapeDtypeStruct((depth, bp, n), F32),
        compiler_params=_params("parallel", "parallel"),
        name="ada_mod",
    )(c_all, w_ada, b_ada.reshape(depth, 1, n))


C_QM = 0
C_LAT = H_MLA * Q_HEAD_COLS
C_KPE = C_LAT + D_LATENT
C_QB = C_KPE + LANES
C_KB = C_QB + H_SB * D_SB
C_VB = C_KB + H_SB * D_SB
C_END = C_VB + H_SB * D_SB


def _rope_pair(v, g, cos, sin):
    ssq = jnp.sum(v * v, axis=-1, keepdims=True) * (0.5 / D_ROPE)
    vn = v * lax.rsqrt(ssq + EPS) * g
    return vn * cos + pltpu.roll(vn, D_ROPE, 1) * sin


N_INPROJ_IN = 10


def _inproj_kernel(*refs, seg, nseg, d):
    x_ref, mod_ref, gn_ref, w_ref, gqn_ref, gqr_ref, glat_ref, gkr_ref, cos_ref, sin_ref = refs[:N_INPROJ_IN]
    qm_ref, lat_ref, kpe_ref, kpe128_ref, qb_ref, kb_ref, vb_ref, kb16_ref, vb16_ref, h_scr = refs[-10:]
    for s in range(nseg):
        rows = slice(s * seg, (s + 1) * seg)
        m = mod_ref[s]
        h = _rms_rows(x_ref[rows, :], gn_ref[...]) * (1.0 + m[:, d:2 * d]) + m[:, 0:d]
        h_scr[rows, :] = h.astype(BF16)
    h = h_scr[...]
    cos = cos_ref[...]
    sin = sin_ref[...]

    for hd in range(H_MLA):
        c0 = C_QM + hd * Q_HEAD_COLS
        qh = _dot(h, w_ref[:, c0:c0 + Q_HEAD_COLS])
        qn = _rms_rows(qh[:, :D_NOPE], gqn_ref[...]) * MLA_Q_SCALE
        qp = _rope_pair(qh[:, D_NOPE:], gqr_ref[...], cos, sin) * MLA_Q_SCALE
        qm_ref[:, c0:c0 + D_NOPE] = qn.astype(BF16)
        qm_ref[:, c0 + D_NOPE:c0 + Q_HEAD_COLS] = qp.astype(BF16)

    lat_ref[...] = _rms_rows(_dot(h, w_ref[:, C_LAT:C_KPE]), glat_ref[...])

    kp = _rope_pair(_dot(h, w_ref[:, C_KPE:C_QB]), gkr_ref[...], cos, sin)
    kpe_ref[...] = kp[:, :D_ROPE]
    lane = lax.broadcasted_iota(jnp.int32, kp.shape, 1)
    kpe128_ref[...] = jnp.where(lane < D_ROPE, kp, 0.0).astype(BF16)

    qb_ref[...] = (_dot(h, w_ref[:, C_QB:C_KB]) * SB_SCALE).astype(BF16)
    n = h.shape[0]
    kb = _dot(h, w_ref[:, C_KB:C_VB])
    kb16_ref[...] = kb.astype(BF16)
    vb = _dot(h, w_ref[:, C_VB:C_END])
    vb16_ref[...] = vb.astype(BF16)
    for hd in range(H_SB):
        head_rows = pl.ds(hd, n, stride=H_SB)
        kb_ref[head_rows, :] = kb[:, hd * D_SB:(hd + 1) * D_SB]
        vb_ref[head_rows, :] = vb[:, hd * D_SB:(hd + 1) * D_SB]


def _row_tiling(batch, t, tm_max):
    if t >= tm_max:
        assert t % tm_max == 0
        return tm_max, tm_max, 1
    nseg = max(1, min(batch, tm_max // t))
    while batch % nseg:
        nseg -= 1
    return nseg * t, t, nseg


def _mod_spec(t, tm, nseg, width):
    if nseg == 1:
        per = t // tm
        return pl.BlockSpec((1, 1, width), lambda i: (i // per, 0, 0))
    return pl.BlockSpec((nseg, 1, width), lambda i: (i, 0, 0))


STACKED_OUTS = (1, 2, 5, 6)


def _inproj(x2, mod3, gn, w_p, gqn, gqr, glat, gkr, cos, sin, *, batch, t, tm_max, layer, depth, stacks):
    r, d = x2.shape
    tm, seg, nseg = _row_tiling(batch, t, tm_max)
    full = lambda shape: pl.BlockSpec(shape, lambda i: (0,) * len(shape))
    rows = lambda w: pl.BlockSpec((tm, w), lambda i: (i, 0))
    sb = H_SB * D_SB
    outs = ((1, H_MLA * Q_HEAD_COLS, BF16),
            (1, D_LATENT, F32),
            (1, D_ROPE, F32),
            (1, LANES, BF16),
            (1, sb, BF16),
            (H_SB, D_SB, F32), (H_SB, D_SB, F32),
            (1, sb, BF16), (1, sb, BF16))
    out_shapes, out_specs = [], []
    for k, (per, w, dt) in enumerate(outs):
        if k in STACKED_OUTS:
            out_shapes.append(jax.ShapeDtypeStruct((depth, r * per, w), dt))
            out_specs.append(pl.BlockSpec((None, tm * per, w), lambda i: (layer, i, 0)))
        else:
            out_shapes.append(jax.ShapeDtypeStruct((r, w), dt))
            out_specs.append(rows(w))
    in_specs = [rows(d), _mod_spec(t, tm, nseg, 6 * d), full((1, d)), full(w_p.shape),
                full((1, D_NOPE)), full((1, LANES)), full((1, D_LATENT)), full((1, LANES)),
                rows(LANES), rows(LANES)]
    operands = (x2, mod3, gn, w_p, gqn, gqr, glat, gkr, cos, sin)
    assert len(operands) == N_INPROJ_IN
    aliases = {}
    if stacks is not None:
        in_specs += [pl.BlockSpec(memory_space=pl.ANY)] * len(STACKED_OUTS)
        operands += tuple(stacks)
        aliases = {N_INPROJ_IN + n: k for n, k in enumerate(STACKED_OUTS)}
    return pl.pallas_call(
        functools.partial(_inproj_kernel, seg=seg, nseg=nseg, d=d),
        grid=(r // tm,),
        in_specs=in_specs,
        out_specs=out_specs,
        out_shape=tuple(out_shapes),
        input_output_aliases=aliases,
        scratch_shapes=[pltpu.VMEM((tm, d), BF16)],
        compiler_params=_params("parallel"),
        name="in_proj",
    )(*operands)


def _kvup_kernel(lat_ref, wuk_ref, wuv_ref, gkn_ref, k_ref, v_ref, *, v_transposed):
    l16 = lat_ref[...].astype(BF16)
    kn = _dot(l16, wuk_ref[...])
    for hd in range(H_MLA):
        cols = slice(hd * D_NOPE, (hd + 1) * D_NOPE)
        k_ref[:, cols] = _rms_rows(kn[:, cols], gkn_ref[...]).astype(BF16)
    if v_transposed:
        vt = _dot_nt(wuv_ref[...], l16).astype(BF16)
        ones = jnp.ones((ONES_ROWS, vt.shape[1]), BF16)
        for hd in range(H_MLA):
            v_ref[hd * VT_ROWS:hd * VT_ROWS + D_VA, :] = vt[hd * D_VA:(hd + 1) * D_VA, :]
            v_ref[hd * VT_ROWS + D_VA:(hd + 1) * VT_ROWS, :] = ones
    else:
        v_ref[...] = _dot(l16, wuv_ref[...]).astype(BF16)


def _kvup(lat_all, wuk, wuv, gkn, *, tm_max, per_batch_t=None, row0=0, n_rows=None):
    r = lat_all.shape[0] if n_rows is None else n_rows
    tm = min(tm_max, r if per_batch_t is None else per_batch_t)
    while r % tm or row0 % tm:
        tm //= 2
    off = row0 // tm
    full = lambda shape: pl.BlockSpec(shape, lambda i: (0,) * len(shape))
    rows = lambda w: pl.BlockSpec((tm, w), lambda i: (i, 0))
    vcols = H_MLA * D_VA
    if per_batch_t is None:
        v_spec, v_shape = rows(vcols), jax.ShapeDtypeStruct((r, vcols), BF16)
    else:
        per = per_batch_t // tm
        v_spec = pl.BlockSpec((None, H_MLA * VT_ROWS, tm), lambda i: (i // per, 0, i % per))
        v_shape = jax.ShapeDtypeStruct((r // per_batch_t, H_MLA * VT_ROWS, per_batch_t), BF16)
    return pl.pallas_call(
        functools.partial(_kvup_kernel, v_transposed=per_batch_t is not None),
        grid=(r // tm,),
        in_specs=[pl.BlockSpec((tm, D_LATENT), lambda i: (i + off, 0)),
                  full(wuk.shape), full(wuv.shape), full((1, D_NOPE))],
        out_specs=[rows(H_MLA * D_NOPE), v_spec],
        out_shape=(jax.ShapeDtypeStruct((r, H_MLA * D_NOPE), BF16), v_shape),
        compiler_params=_params("parallel"),
        name="kv_up",
    )(lat_all, wuk, wuv, gkn)


def _mla_cached_kernel(q_ref, knp_ref, kpep_ref, vp_ref, knn_ref, kpen_ref, vn_ref, o_ref,
                       m_scr, l_scr, acc_scr, *, t, tk, past):
    m_scr[...] = jnp.full(m_scr.shape, MASK_VALUE, F32)
    l_scr[...] = jnp.zeros(l_scr.shape, F32)
    acc_scr[...] = jnp.zeros(acc_scr.shape, F32)
    qk = D_NOPE + D_ROPE

    def update(kn, kpe, v, mask):
        ss = []
        for hd in range(H_MLA):
            kb = jnp.concatenate([kn[:, hd * D_NOPE:(hd + 1) * D_NOPE], kpe], axis=-1)
            ss.append(_dot_nt(kb, q_ref[:, hd * Q_HEAD_COLS:hd * Q_HEAD_COLS + qk]))
        for hd, s in enumerate(ss):
            if mask is not None:
                s = jnp.where(mask, s, MASK_VALUE)
            m_old = m_scr[hd]
            m_new = jnp.maximum(m_old, jnp.max(s, axis=0, keepdims=True))
            alpha = jnp.exp2(m_old - m_new)
            p = jnp.exp2(s - m_new)
            l_scr[hd] = alpha * l_scr[hd] + jnp.sum(p, axis=0, keepdims=True)
            acc_scr[hd] = alpha * acc_scr[hd] + _dot_tn(v[:, hd * D_VA:(hd + 1) * D_VA], p.astype(BF16))
            m_scr[hd] = m_new

    def body(j, carry):
        rows = pl.ds(pl.multiple_of(j * tk, tk), tk)
        update(knp_ref[rows, :], kpep_ref[rows, :].astype(BF16), vp_ref[rows, :], None)
        return carry

    lax.fori_loop(0, past // tk, body, 0)

    if t > CHUNK:
        kc = lax.broadcasted_iota(jnp.int32, (t, t), 0) // CHUNK
        qc = lax.broadcasted_iota(jnp.int32, (t, t), 1) // CHUNK
        mask = kc <= qc
    else:
        mask = None
    update(knn_ref[...], kpen_ref[:, 0:D_ROPE], vn_ref[...], mask)

    for hd in range(H_MLA):
        o = acc_scr[hd] * (1.0 / l_scr[hd])
        o_ref[:, hd * D_VA:(hd + 1) * D_VA] = o.T


def _mla_attn_cached(q, knp, kpe_cache, vp, knn, kpen, vn, *, layer, batch, t, past, tk):
    assert past % tk == 0 and past % CHUNK == 0 and t % CHUNK == 0
    vcols = H_MLA * D_VA
    new = lambda a: pl.BlockSpec((t, a.shape[1]), lambda b: (b, 0))
    old = lambda a: pl.BlockSpec((past, a.shape[1]), lambda b: (b, 0))
    scratch = [pltpu.VMEM((H_MLA, 1, t), F32), pltpu.VMEM((H_MLA, 1, t), F32), pltpu.VMEM((H_MLA, D_VA, t), F32)]
    return pl.pallas_call(
        functools.partial(_mla_cached_kernel, t=t, tk=tk, past=past),
        grid=(batch,),
        in_specs=[new(q), old(knp),
                  pl.BlockSpec((None, None, past, D_ROPE), lambda b: (layer, b, 0, 0)),
                  old(vp), new(knn), new(kpen), new(vn)],
        out_specs=pl.BlockSpec((t, vcols), lambda b: (b, 0)),
        out_shape=jax.ShapeDtypeStruct((batch * t, vcols), F32),
        scratch_shapes=scratch,
        compiler_params=_params("parallel"),
        name="mla_attn_cached",
    )(q, knp, kpe_cache, vp, knn, kpen, vn)


def _mla_first_kernel(q_ref, kn_ref, kpe_ref, vt_ref, o_ref, qt_scr, m_scr, acc_scr, s_scr, *, tq, tk):
    i = pl.program_id(1)
    nb = tq // tk
    assert nb == 2
    for hd in range(H_MLA):
        qt_scr[hd] = q_ref[:, hd * Q_HEAD_COLS:(hd + 1) * Q_HEAD_COLS].T
    m_scr[...] = jnp.full(m_scr.shape, MASK_VALUE, F32)
    acc_scr[...] = jnp.zeros(acc_scr.shape, F32)
    last = i * nb + nb - 1

    def scores(b, buf):
        r0 = pl.multiple_of(b * tk, tk)
        kpe = kpe_ref[pl.ds(r0, tk), :]
        for hd in range(H_MLA):
            kb = jnp.concatenate([kn_ref[pl.ds(r0, tk), hd * D_NOPE:(hd + 1) * D_NOPE], kpe], axis=-1)
            s_scr[buf, hd] = _dot(kb, qt_scr[hd])

    def values(b, buf, own):
        r0 = pl.multiple_of(b * tk, tk)
        for hd in range(H_MLA):
            vt = vt_ref[hd * VT_ROWS:(hd + 1) * VT_ROWS, pl.ds(r0, tk)]
            for w0 in range(0, tq, MXU_COLS):
                ps, alphas = [], []
                for c0 in range(w0, w0 + MXU_COLS, LANES):
                    qs = slice(c0, c0 + LANES)
                    s = s_scr[buf, hd, :, qs]
                    if own is not None:
                        kc = lax.broadcasted_iota(jnp.int32, (tk, LANES), 0) // CHUNK + own * (tk // CHUNK)
                        qc = (lax.broadcasted_iota(jnp.int32, (tk, LANES), 1) + c0) // CHUNK
                        s = jnp.where(kc <= qc, s, MASK_VALUE)
                    m_old = m_scr[hd, :, qs]
                    m_new = jnp.maximum(m_old, jnp.max(s, axis=0, keepdims=True))
                    m_scr[hd, :, qs] = m_new
                    alphas.append(jnp.exp2(m_old - m_new))
                    ps.append(jnp.exp2(s - m_new).astype(BF16))
                wide = slice(w0, w0 + MXU_COLS)
                acc_scr[hd, :, wide] = (jnp.concatenate(alphas, axis=1) * acc_scr[hd, :, wide]
                                        + _dot(vt, jnp.concatenate(ps, axis=1)))

    def step(t, buf, own=None):
        scores(t, buf)
        values(t - 1, 1 - buf, own)

    scores(0, 0)

    def pair(k, carry):
        t = 1 + 2 * k
        step(t, 1)
        step(t + 1, 0)
        return carry

    def two_pairs(k, carry):
        return pair(2 * k + 1, pair(2 * k, carry))

    lax.fori_loop(0, i // 2, two_pairs, 0)
    lax.fori_loop(i // 2 * 2, i, pair, 0)
    step(last, 1, own=0)
    values(last, 1, 1)

    for hd in range(H_MLA):
        o = acc_scr[hd, 0:D_VA, :] * (1.0 / acc_scr[hd, D_VA:D_VA + 1, :])
        o_ref[:, hd * D_VA:(hd + 1) * D_VA] = o.T


def _mla_attn_first(q, kn, kpe, vt, *, batch, t, tq, tk):
    assert t % tq == 0 and tq == 2 * tk and tk % CHUNK == 0
    nq = t // tq
    vcols = H_MLA * D_VA
    resident = lambda a: pl.BlockSpec((t, a.shape[1]), lambda b, i: (b, 0), pipeline_mode=pl.Buffered(1))
    scratch = [pltpu.VMEM((H_MLA, Q_HEAD_COLS, tq), BF16),
               pltpu.VMEM((H_MLA, 1, tq), F32),
               pltpu.VMEM((H_MLA, VT_ROWS, tq), F32),
               pltpu.VMEM((2, H_MLA, tk, tq), F32)]
    return pl.pallas_call(
        functools.partial(_mla_first_kernel, tq=tq, tk=tk),
        grid=(batch, nq),
        in_specs=[pl.BlockSpec((tq, q.shape[1]), lambda b, i: (b * nq + i, 0)), resident(kn), resident(kpe),
                  pl.BlockSpec((None, H_MLA * VT_ROWS, t), lambda b, i: (b, 0, 0), pipeline_mode=pl.Buffered(1))],
        out_specs=pl.BlockSpec((tq, vcols), lambda b, i: (b * nq + i, 0)),
        out_shape=jax.ShapeDtypeStruct((batch * t, vcols), F32),
        scratch_shapes=scratch,
        compiler_params=_params("parallel", "arbitrary"),
        name="mla_attn_first",
    )(q, kn, kpe, vt)


def _sb_kernel(q_ref, *refs, tq, tk, past):
    if past:
        kp_ref, vp_ref, k_ref, v_ref, o_ref, c_scr, acc_scr = refs
    else:
        k_ref, v_ref, o_ref, c_scr, acc_scr = refs
    i = pl.program_id(1)
    c_scr[...] = jnp.zeros(c_scr.shape, F32)
    acc_scr[...] = jnp.zeros(acc_scr.shape, F32)
    heads = [slice(hd * D_SB, (hd + 1) * D_SB) for hd in range(H_SB)]

    def later_eq(n):
        return (lax.broadcasted_iota(jnp.int32, (n, n), 1)
                >= lax.broadcasted_iota(jnp.int32, (n, n), 0)).astype(BF16)

    def new_rows(r0, n):
        return ([k_ref[pl.ds(r0, n), cols] for cols in heads], [v_ref[pl.ds(r0, n), cols] for cols in heads])

    def cached_rows(r0, n):
        return ([kp_ref[pl.ds(r0, n), hd, :].astype(BF16) for hd in range(H_SB)],
                [vp_ref[pl.ds(r0, n), hd, :].astype(BF16) for hd in range(H_SB)])

    def update(kv, n, tri, before):
        ks, vs = kv
        zs = [_dot_nt(ks[hd], q_ref[:, cols]) for hd, cols in enumerate(heads)]
        es, incls = [], []
        for z in zs:
            ls = jnp.minimum(-z, 0.0) - jnp.log(1.0 + jnp.exp(-jnp.abs(z)))
            lsm = ls if before is None else jnp.where(before, ls, 0.0)
            incl = _dot(tri, lsm.astype(BF16))
            incls.append(incl)
            es.append(z + incl if before is None else z + ls + (incl - lsm))
        top = None
        for hd in range(H_SB):
            c = c_scr[hd]
            w = jnp.exp(es[hd] + c)
            if before is not None:
                w = jnp.where(before, w, 0.0)
            acc_scr[hd] += _dot_tn(vs[hd], w.astype(BF16))
            c_new = c + incls[hd][0:1, :]
            c_scr[hd] = c_new
            top_h = jnp.max(c_new)
            top = top_h if top is None else jnp.maximum(top, top_h)
        return top

    own0 = i * tq
    before = lax.broadcasted_iota(jnp.int32, (tq, tq), 0) < lax.broadcasted_iota(jnp.int32, (tq, tq), 1)
    top = update(new_rows(pl.multiple_of(own0, tq), tq), tq, later_eq(tq), before)
    tri_k = later_eq(tk)
    older = cached_rows if past else new_rows

    def cond(st):
        j, top = st
        return jnp.logical_and(j >= 0, top > SB_EXP_ZERO)

    def body(st):
        j, _ = st
        return j - 1, update(older(pl.multiple_of(j * tk, tk), tk), tk, tri_k, None)

    lax.while_loop(cond, body, ((past + own0) // tk - 1, top))
    for hd in range(H_SB):
        o_ref[:, hd * D_SB:(hd + 1) * D_SB] = acc_scr[hd].T


def _sb_attn(q, k, v, *, batch, t, tq, tk, past=0, k_cache=None, v_cache=None, layer=0):
    assert t % tq == 0 and past % tk == 0 and (tq == tk if past == 0 else t == tq)
    nq = t // tq
    cols = H_SB * D_SB
    q_spec = pl.BlockSpec((tq, cols), lambda b, i: (b * nq + i, 0))
    if past:
        cache = pl.BlockSpec((None, None, past, H_SB, D_SB), lambda b, i: (layer, b, 0, 0, 0))
        new = pl.BlockSpec((t, cols), lambda b, i: (b, 0))
        in_specs, operands = [q_spec, cache, cache, new, new], (q, k_cache, v_cache, k, v)
    else:
        resident = pl.BlockSpec((t, cols), lambda b, i: (b, 0), pipeline_mode=pl.Buffered(1))
        in_specs, operands = [q_spec, resident, resident], (q, k, v)
    return pl.pallas_call(
        functools.partial(_sb_kernel, tq=tq, tk=tk, past=past),
        grid=(batch, nq),
        in_specs=in_specs,
        out_specs=q_spec,
        out_shape=jax.ShapeDtypeStruct((batch * t, cols), F32),
        scratch_shapes=[pltpu.VMEM((H_SB, 1, tq), F32), pltpu.VMEM((H_SB, D_SB, tq), F32)],
        compiler_params=_params("parallel", "arbitrary"),
        name="sb_attn",
    )(*operands)


def _post_kernel(oa_ref, ob_ref, x_ref, mod_ref, goa_ref, gob_ref, gnf_ref, wout_ref, wg_ref, wu_ref, wd_ref,
                 wconv_ref, bconv_ref, cbuf_ref, y_ref, nconv_ref, x1_scr, h_scr, u_scr,
                 *, seg, nseg, d, per_batch):
    i = pl.program_id(0)
    na = oa_ref.shape[1]
    ma = _rms_rows(oa_ref[...], goa_ref[...]).astype(BF16)
    mb = _rms_rows(ob_ref[...], gob_ref[...]).astype(BF16)
    mix = _dot(ma, wout_ref[0:na, :]) + _dot(mb, wout_ref[na:, :])
    for s in range(nseg):
        rows = slice(s * seg, (s + 1) * seg)
        m = mod_ref[s]
        x1 = x_ref[rows, :] + m[:, 2 * d:3 * d] * mix[rows, :]
        x1_scr[rows, :] = x1
        h = _rms_rows(x1, gnf_ref[...]) * (1.0 + m[:, 4 * d:5 * d]) + m[:, 3 * d:4 * d]
        h_scr[rows, :] = h.astype(BF16)
    h = h_scr[...]

    if nseg == 1:
        first = i % per_batch == 0

        @pl.when(first)
        def _():
            u_scr[0, 8 - (CONV_W - 1):8, :] = cbuf_ref[0]

        @pl.when(jnp.logical_not(first))
        def _():
            u_scr[0, 0:8, :] = u_scr[0, seg:seg + 8, :]
    else:
        for s in range(nseg):
            u_scr[s, 8 - (CONV_W - 1):8, :] = cbuf_ref[s]

    ff = wg_ref.shape[1]
    bounds = [ff * c // FF_CHUNKS // MXU_COLS * MXU_COLS for c in range(FF_CHUNKS)] + [ff]
    chunks = [slice(bounds[c], bounds[c + 1]) for c in range(FF_CHUNKS)]
    us = [_dot(h, wg_ref[:, c]) for c in chunks]
    ups = [_dot(h, wu_ref[:, c]) for c in chunks]
    down = None
    for c, u, up in zip(chunks, us, ups):
        for s in range(nseg):
            u_scr[s, 8:8 + seg, c] = u[s * seg:(s + 1) * seg, :]
            nconv_ref[s, :, c] = u[(s + 1) * seg - (CONV_W - 1):(s + 1) * seg, :]
        parts = []
        for s in range(nseg):
            uc = bconv_ref[:, c]
            for tap in range(CONV_W):
                lo = 8 - (CONV_W - 1) + tap
                uc = uc + u_scr[s, lo:lo + seg, c] * wconv_ref[tap:tap + 1, c]
            parts.append(uc)
        uc = parts[0] if nseg == 1 else jnp.concatenate(parts, axis=0)
        act = (uc * jax.nn.sigmoid(uc) * up).astype(BF16)
        part = _dot(act, wd_ref[c, :])
        down = part if down is None else down + part
    for s in range(nseg):
        rows = slice(s * seg, (s + 1) * seg)
        y_ref[rows, :] = x1_scr[rows, :] + mod_ref[s][:, 5 * d:6 * d] * down[rows, :]


def _post(oa, ob, x2, mod3, goa, gob, gnf, wout, wg, wu, wd, wconv, bconv, cbuf, *, batch, t, tm_max):
    r, d = x2.shape
    ff = wg.shape[1]
    tm, seg, nseg = _row_tiling(batch, t, tm_max)
    per_batch = max(1, t // tm)
    full = lambda shape: pl.BlockSpec(shape, lambda i: (0,) * len(shape))
    rows = lambda w: pl.BlockSpec((tm, w), lambda i: (i, 0))
    if nseg == 1:
        state_spec = pl.BlockSpec((1, CONV_W - 1, ff), lambda i: (i // per_batch, 0, 0))
    else:
        state_spec = pl.BlockSpec((nseg, CONV_W - 1, ff), lambda i: (i, 0, 0))
    return pl.pallas_call(
        functools.partial(_post_kernel, seg=seg, nseg=nseg, d=d, per_batch=per_batch),
        grid=(r // tm,),
        in_specs=[rows(oa.shape[1]), rows(ob.shape[1]), rows(d), _mod_spec(t, tm, nseg, 6 * d),
                  full((1, oa.shape[1])), full((1, ob.shape[1])), full((1, d)),
                  full(wout.shape), full(wg.shape), full(wu.shape), full(wd.shape),
                  full((CONV_W, ff)), full((1, ff)), state_spec],
        out_specs=[rows(d), state_spec],
        out_shape=(jax.ShapeDtypeStruct((r, d), F32),
                   jax.ShapeDtypeStruct((batch, CONV_W - 1, ff), F32)),
        scratch_shapes=[pltpu.VMEM((tm, d), F32), pltpu.VMEM((tm, d), BF16),
                        pltpu.VMEM((nseg, seg + 8, ff), F32)],
        compiler_params=_params("arbitrary"),
        name="post_ffn",
    )(oa, ob, x2, mod3, goa, gob, gnf, wout, wg, wu, wd, wconv, bconv, cbuf)


def _in_col_runs():
    qa = H_MLA * (D_NOPE + D_ROPE)
    half = D_ROPE // 2
    runs = []
    for h in range(H_MLA):
        base = h * (D_NOPE + D_ROPE)
        pe = base + D_NOPE
        runs += [(base, pe + D_ROPE), (pe + half, pe + D_ROPE), (pe, pe + half)]
    runs.append((qa, qa + D_LATENT))
    kp = qa + D_LATENT
    runs += [(kp, kp + D_ROPE), (kp + half, kp + D_ROPE), (kp, kp + half)]
    runs.append((kp + D_ROPE, kp + D_ROPE + 3 * H_SB * D_SB))
    return runs


def _win_kernel(w_ref, o_ref):
    w = w_ref[...]
    o_ref[...] = jnp.concatenate([w[:, a:b] for a, b in _in_col_runs()], axis=1).astype(o_ref.dtype)


def _permute_w_in(w_in):
    depth, d, n = w_in.shape
    x = w_in.reshape(depth * d, n)
    tm = 256
    out = pl.pallas_call(
        _win_kernel,
        grid=(depth * d // tm,),
        in_specs=[pl.BlockSpec((tm, n), lambda i: (i, 0))],
        out_specs=pl.BlockSpec((tm, C_END), lambda i: (i, 0)),
        out_shape=jax.ShapeDtypeStruct((depth * d, C_END), BF16),
        compiler_params=_params("parallel"),
        name="permute_w_in",
    )(x)
    return out.reshape(depth, d, C_END)


def _rope_gain(g):
    half = D_ROPE // 2
    return jnp.concatenate([g, g[..., half:], g[..., :half]], axis=-1)


def _rope_tables(pos):
    half = D_ROPE // 2
    freqs = ROPE_THETA ** (-jnp.arange(half, dtype=F32) / half)
    ang = pos.astype(F32)[:, None] * freqs[None, :]
    cos, sin = jnp.cos(ang), jnp.sin(ang)
    return jnp.tile(cos, (1, 4)), jnp.concatenate([-sin, sin, sin, -sin], axis=-1)


def kernel(x_prompt, x_sample, c_prompt, c_sample, cache_mla_latent, cache_mla_krope, cache_sb_k, cache_sb_v, state_ffn_conv, w_ada, b_ada, g_norm_mix, g_norm_ffn, w_in, g_kv_latent, g_q_nope, g_q_rope, g_k_nope, g_k_rope, w_uk, w_uv, g_out_mla, g_out_sb, w_out, w_gate, w_up, w_conv, b_conv, w_down):
    depth = w_in.shape[0]
    bp, tp, d = x_prompt.shape
    bs, ts, _ = x_sample.shape
    past = cache_mla_latent.shape[2]
    ff = w_gate.shape[2]
    sb = H_SB * D_SB

    nb = bp + bs
    nb_pad = -(-nb // 8) * 8
    c_all = jnp.concatenate([c_prompt, c_sample, jnp.zeros((nb_pad - nb, d), F32)], axis=0)
    mod = _ada_all(c_all, w_ada, b_ada)

    w_in_p = _permute_w_in(w_in)
    w_uk16, w_uv16, w_out16 = _to_bf16(w_uk), _to_bf16(w_uv), _to_bf16(w_out)
    w_uvt16 = jnp.swapaxes(w_uv16, 1, 2)
    w_gate16, w_up16, w_down16 = _to_bf16(w_gate), _to_bf16(w_up), _to_bf16(w_down)
    gqr, gkr = _rope_gain(g_q_rope), _rope_gain(g_k_rope)

    cos_p, sin_p = _rope_tables(jnp.arange(tp, dtype=jnp.int32))
    cos_p, sin_p = jnp.tile(cos_p, (bp, 1)), jnp.tile(sin_p, (bp, 1))
    cos_s, sin_s = _rope_tables(past + jnp.arange(ts, dtype=jnp.int32))
    cos_s, sin_s = jnp.tile(cos_s, (bs, 1)), jnp.tile(sin_s, (bs, 1))

    lat_cache_rows = cache_mla_latent.reshape(-1, D_LATENT)

    def layer(l, x2, mod3, cos, sin, batch, t, p, stacks):
        row = lambda a: a[l][None, :]
        qm, lat_st, kpe_st, kpe128, qb, kb_st, vb_st, kb16, vb16 = _inproj(
            x2, mod3, row(g_norm_mix), w_in_p[l], row(g_q_nope), row(gqr), row(g_kv_latent), row(gkr),
            cos, sin, batch=batch, t=t, tm_max=512, layer=l, depth=depth, stacks=stacks)
        new_lat = dict(row0=l * batch * t, n_rows=batch * t)
        lat_rows = lat_st.reshape(-1, D_LATENT)
        if p:
            cbuf = state_ffn_conv[l]
            knp, vp = _kvup(lat_cache_rows, w_uk16[l], w_uv16[l], row(g_k_nope), tm_max=512,
                            row0=l * batch * p, n_rows=batch * p)
            knn, vn = _kvup(lat_rows, w_uk16[l], w_uv16[l], row(g_k_nope), tm_max=512, **new_lat)
            oa = _mla_attn_cached(qm, knp, cache_mla_krope, vp, knn, kpe128, vn,
                                  layer=l, batch=batch, t=t, past=p, tk=256)
            ob = _sb_attn(qb, kb16, vb16, batch=batch, t=t, tq=t, tk=256, past=p,
                          k_cache=cache_sb_k, v_cache=cache_sb_v, layer=l)
        else:
            cbuf = jnp.zeros((batch, CONV_W - 1, ff), F32)
            kn, vt = _kvup(lat_rows, w_uk16[l], w_uvt16[l], row(g_k_nope), tm_max=512, per_batch_t=t, **new_lat)
            oa = _mla_attn_first(qm, kn, kpe128, vt, batch=batch, t=t, tq=512, tk=256)
            ob = _sb_attn(qb, kb16, vb16, batch=batch, t=t, tq=256, tk=256)
        y, nconv = _post(oa, ob, x2, mod3, row(g_out_mla), row(g_out_sb), row(g_norm_ffn), w_out16[l],
                         w_gate16[l], w_up16[l], w_down16[l], w_conv[l], row(b_conv), cbuf,
                         batch=batch, t=t, tm_max=256)
        return y, (lat_st, kpe_st, kb_st, vb_st), nconv

    xp = x_prompt.reshape(bp * tp, d)
    xs = x_sample.reshape(bs * ts, d)
    st_p = st_s = None
    conv_p, conv_s = [], []
    for l in range(depth):
        mod_p = mod[l, :bp].reshape(bp, 1, 6 * d)
        mod_s = mod[l, bp:nb].reshape(bs, 1, 6 * d)
        xp, st_p, nc_p = layer(l, xp, mod_p, cos_p, sin_p, bp, tp, 0, st_p)
        xs, st_s, nc_s = layer(l, xs, mod_s, cos_s, sin_s, bs, ts, past, st_s)
        conv_p.append(nc_p)
        conv_s.append(nc_s)

    def step_outputs(stacks, convs, batch, t):
        lat_st, kpe_st, kb_st, vb_st = stacks
        return (lat_st.reshape(depth, batch, t, D_LATENT), kpe_st.reshape(depth, batch, t, D_ROPE),
                kb_st.reshape(depth, batch, t, H_SB, D_SB), vb_st.reshape(depth, batch, t, H_SB, D_SB),
                jnp.stack(convs))

    return ((xp.reshape(bp, tp, d), xs.reshape(bs, ts, d))
            + step_outputs(st_p, conv_p, bp, tp) + step_outputs(st_s, conv_s, bs, ts))
```

```python
import functools
import math

import jax
import jax.numpy as jnp
from jax import lax
from jax.experimental import pallas as pl
from jax.experimental.pallas import tpu as pltpu

F32 = jnp.float32
BF16 = jnp.bfloat16

H_MLA = 4
D_NOPE = 128
D_ROPE = 64
D_VA = 128
D_LATENT = 512
H_SB = 4
D_SB = 128
CONV_W = 3
CHUNK = 64
ROPE_THETA = 10000.0
EPS = 1e-6

LANES = 128
MXU_COLS = 256
Q_HEAD_COLS = 2 * LANES
MASK_VALUE = -1e30
FF_CHUNKS = 2
ONES_ROWS = 16
VT_ROWS = D_VA + ONES_ROWS
SB_EXP2_ZERO = -152.0
VMEM_LIMIT_BYTES = 56 * 1024 * 1024

MLA_Q_SCALE = math.log2(math.e) * (D_NOPE + D_ROPE) ** -0.5
SB_Q_SCALE = math.log2(math.e) * D_SB ** -0.5


def _rms_rows(x, g):
    return x * lax.rsqrt(jnp.mean(x * x, axis=-1, keepdims=True) + EPS) * g


def _dot(a, b):
    return jnp.dot(a, b, preferred_element_type=F32)


def _dot_nt(a, b):
    return lax.dot_general(a, b, (((1,), (1,)), ((), ())), preferred_element_type=F32)


def _dot_tn(a, b):
    return lax.dot_general(a, b, (((0,), (0,)), ((), ())), preferred_element_type=F32)


def _layer_spec(w, layer):
    return pl.BlockSpec((None,) + w.shape[1:], lambda *_: (layer, 0, 0), pipeline_mode=pl.Buffered(1))


def _params(*sem):
    return pltpu.CompilerParams(dimension_semantics=sem, vmem_limit_bytes=VMEM_LIMIT_BYTES)


def _cast_kernel(x_ref, o_ref):
    o_ref[...] = x_ref[...].astype(o_ref.dtype)


def _to_bf16(w):
    x = w.reshape(-1, w.shape[-1])
    r, c = x.shape
    tm = 512
    while r % tm:
        tm //= 2
    out = pl.pallas_call(
        _cast_kernel,
        grid=(r // tm,),
        in_specs=[pl.BlockSpec((tm, c), lambda i: (i, 0))],
        out_specs=pl.BlockSpec((tm, c), lambda i: (i, 0)),
        out_shape=jax.ShapeDtypeStruct((r, c), BF16),
        compiler_params=_params("parallel"),
        name="cast_bf16",
    )(x)
    return out.reshape(w.shape)


def _ada_kernel(c_ref, w_ref, b_ref, o_ref):
    c = c_ref[...]
    s = (c * jax.nn.sigmoid(c)).astype(BF16)
    o_ref[0] = _dot(s, w_ref[0].astype(BF16)) + b_ref[0]


def _ada_all(c_all, w_ada, b_ada):
    depth, d, n = w_ada.shape
    bp = c_all.shape[0]
    tn = d
    return pl.pallas_call(
        _ada_kernel,
        grid=(depth, n // tn),
        in_specs=[
            pl.BlockSpec((bp, d), lambda l, j: (0, 0)),
            pl.BlockSpec((1, d, tn), lambda l, j: (l, 0, j)),
            pl.BlockSpec((1, 1, tn), lambda l, j: (l, 0, j)),
        ],
        out_specs=pl.BlockSpec((1, bp, tn), lambda l, j: (l, 0, j)),
        out_shape=jax.ShapeDtypeStruct((depth, bp, n), F32),
        compiler_params=_params("parallel", "parallel"),
        name="ada_mod",
    )(c_all, w_ada, b_ada.reshape(depth, 1, n))


C_QM = 0
C_LAT = H_MLA * Q_HEAD_COLS
C_KPE = C_LAT + D_LATENT
C_QB = C_KPE + LANES
C_KB = C_QB + H_SB * D_SB
C_VB = C_KB + H_SB * D_SB
C_END = C_VB + H_SB * D_SB


def _rope_pair(v, g, cos, sin):
    ssq = jnp.sum(v * v, axis=-1, keepdims=True) * (0.5 / D_ROPE)
    vn = v * lax.rsqrt(ssq + EPS) * g
    return vn * cos + pltpu.roll(vn, D_ROPE, 1) * sin


N_INPROJ_IN = 10


def _inproj_kernel(*refs, seg, nseg, d):
    x_ref, mod_ref, gn_ref, w_ref, gqn_ref, gqr_ref, glat_ref, gkr_ref, cos_ref, sin_ref = refs[:N_INPROJ_IN]
    qm_ref, lat_ref, kpe_ref, kpe128_ref, qb_ref, kb_ref, vb_ref, kb16_ref, vb16_ref, h_scr = refs[-10:]
    for s in range(nseg):
        rows = slice(s * seg, (s + 1) * seg)
        m = mod_ref[s]
        h = _rms_rows(x_ref[rows, :], gn_ref[...]) * (1.0 + m[:, d:2 * d]) + m[:, 0:d]
        h_scr[rows, :] = h.astype(BF16)
    h = h_scr[...]
    cos = cos_ref[...]
    sin = sin_ref[...]

    for hd in range(H_MLA):
        c0 = C_QM + hd * Q_HEAD_COLS
        qh = _dot(h, w_ref[:, c0:c0 + Q_HEAD_COLS])
        qn = _rms_rows(qh[:, :D_NOPE], gqn_ref[...]) * MLA_Q_SCALE
        qp = _rope_pair(qh[:, D_NOPE:], gqr_ref[...], cos, sin) * MLA_Q_SCALE
        qm_ref[:, c0:c0 + D_NOPE] = qn.astype(BF16)
        qm_ref[:, c0 + D_NOPE:c0 + Q_HEAD_COLS] = qp.astype(BF16)

    lat_ref[...] = _rms_rows(_dot(h, w_ref[:, C_LAT:C_KPE]), glat_ref[...])

    kp = _rope_pair(_dot(h, w_ref[:, C_KPE:C_QB]), gkr_ref[...], cos, sin)
    kpe_ref[...] = kp[:, :D_ROPE]
    lane = lax.broadcasted_iota(jnp.int32, kp.shape, 1)
    kpe128_ref[...] = jnp.where(lane < D_ROPE, kp, 0.0).astype(BF16)

    qb_ref[...] = (_dot(h, w_ref[:, C_QB:C_KB]) * SB_Q_SCALE).astype(BF16)
    n = h.shape[0]
    kb = _dot(h, w_ref[:, C_KB:C_VB])
    kb16_ref[...] = kb.astype(BF16)
    vb = _dot(h, w_ref[:, C_VB:C_END])
    vb16_ref[...] = vb.astype(BF16)
    for hd in range(H_SB):
        head_rows = pl.ds(hd, n, stride=H_SB)
        kb_ref[head_rows, :] = kb[:, hd * D_SB:(hd + 1) * D_SB]
        vb_ref[head_rows, :] = vb[:, hd * D_SB:(hd + 1) * D_SB]


def _row_tiling(batch, t, tm_max):
    if t >= tm_max:
        assert t % tm_max == 0
        return tm_max, tm_max, 1
    nseg = max(1, min(batch, tm_max // t))
    while batch % nseg:
        nseg -= 1
    return nseg * t, t, nseg


def _mod_spec(t, tm, nseg, width):
    if nseg == 1:
        per = t // tm
        return pl.BlockSpec((1, 1, width), lambda i: (i // per, 0, 0))
    return pl.BlockSpec((nseg, 1, width), lambda i: (i, 0, 0))


STACKED_OUTS = (1, 2, 5, 6)


def _inproj(x2, mod3, gn, w_p, gqn, gqr, glat, gkr, cos, sin, *, batch, t, tm_max, layer, depth, stacks):
    r, d = x2.shape
    tm, seg, nseg = _row_tiling(batch, t, tm_max)
    full = lambda shape: pl.BlockSpec(shape, lambda i: (0,) * len(shape))
    rows = lambda w: pl.BlockSpec((tm, w), lambda i: (i, 0))
    sb = H_SB * D_SB
    outs = ((1, H_MLA * Q_HEAD_COLS, BF16),
            (1, D_LATENT, F32),
            (1, D_ROPE, F32),
            (1, LANES, BF16),
            (1, sb, BF16),
            (H_SB, D_SB, F32), (H_SB, D_SB, F32),
            (1, sb, BF16), (1, sb, BF16))
    out_shapes, out_specs = [], []
    for k, (per, w, dt) in enumerate(outs):
        if k in STACKED_OUTS:
            out_shapes.append(jax.ShapeDtypeStruct((depth, r * per, w), dt))
            out_specs.append(pl.BlockSpec((None, tm * per, w), lambda i: (layer, i, 0)))
        else:
            out_shapes.append(jax.ShapeDtypeStruct((r, w), dt))
            out_specs.append(rows(w))
    in_specs = [rows(d), _mod_spec(t, tm, nseg, 6 * d), full((1, d)), _layer_spec(w_p, layer),
                full((1, D_NOPE)), full((1, LANES)), full((1, D_LATENT)), full((1, LANES)),
                rows(LANES), rows(LANES)]
    operands = (x2, mod3, gn, w_p, gqn, gqr, glat, gkr, cos, sin)
    assert len(operands) == N_INPROJ_IN
    aliases = {}
    if stacks is not None:
        in_specs += [pl.BlockSpec(memory_space=pl.ANY)] * len(STACKED_OUTS)
        operands += tuple(stacks)
        aliases = {N_INPROJ_IN + n: k for n, k in enumerate(STACKED_OUTS)}
    return pl.pallas_call(
        functools.partial(_inproj_kernel, seg=seg, nseg=nseg, d=d),
        grid=(r // tm,),
        in_specs=in_specs,
        out_specs=out_specs,
        out_shape=tuple(out_shapes),
        input_output_aliases=aliases,
        scratch_shapes=[pltpu.VMEM((tm, d), BF16)],
        compiler_params=_params("parallel"),
        name="in_proj",
    )(*operands)


def _kvup_kernel(lat_ref, wuk_ref, wuv_ref, gkn_ref, k_ref, v_ref, *, v_transposed):
    l16 = lat_ref[...].astype(BF16)
    kn = _dot(l16, wuk_ref[...])
    for hd in range(H_MLA):
        cols = slice(hd * D_NOPE, (hd + 1) * D_NOPE)
        k_ref[:, cols] = _rms_rows(kn[:, cols], gkn_ref[...]).astype(BF16)
    if v_transposed:
        vt = _dot_nt(wuv_ref[...], l16).astype(BF16)
        ones = jnp.ones((ONES_ROWS, vt.shape[1]), BF16)
        for hd in range(H_MLA):
            v_ref[hd * VT_ROWS:hd * VT_ROWS + D_VA, :] = vt[hd * D_VA:(hd + 1) * D_VA, :]
            v_ref[hd * VT_ROWS + D_VA:(hd + 1) * VT_ROWS, :] = ones
    else:
        v_ref[...] = _dot(l16, wuv_ref[...]).astype(BF16)


def _kvup(lat_all, wuk, wuv, gkn, *, layer, tm_max, per_batch_t=None, row0=0, n_rows=None):
    r = lat_all.shape[0] if n_rows is None else n_rows
    tm = min(tm_max, r if per_batch_t is None else per_batch_t)
    while r % tm or row0 % tm:
        tm //= 2
    off = row0 // tm
    full = lambda shape: pl.BlockSpec(shape, lambda i: (0,) * len(shape))
    rows = lambda w: pl.BlockSpec((tm, w), lambda i: (i, 0))
    vcols = H_MLA * D_VA
    if per_batch_t is None:
        v_spec, v_shape = rows(vcols), jax.ShapeDtypeStruct((r, vcols), BF16)
    else:
        per = per_batch_t // tm
        v_spec = pl.BlockSpec((None, H_MLA * VT_ROWS, tm), lambda i: (i // per, 0, i % per))
        v_shape = jax.ShapeDtypeStruct((r // per_batch_t, H_MLA * VT_ROWS, per_batch_t), BF16)
    return pl.pallas_call(
        functools.partial(_kvup_kernel, v_transposed=per_batch_t is not None),
        grid=(r // tm,),
        in_specs=[pl.BlockSpec((tm, D_LATENT), lambda i: (i + off, 0)),
                  _layer_spec(wuk, layer), _layer_spec(wuv, layer), full((1, D_NOPE))],
        out_specs=[rows(H_MLA * D_NOPE), v_spec],
        out_shape=(jax.ShapeDtypeStruct((r, H_MLA * D_NOPE), BF16), v_shape),
        compiler_params=_params("parallel"),
        name="kv_up",
    )(lat_all, wuk, wuv, gkn)


def _mla_cached_kernel(q_ref, knp_ref, kpep_ref, vp_ref, knn_ref, kpen_ref, vn_ref, o_ref,
                       m_scr, l_scr, acc_scr, *, t, tk, past):
    m_scr[...] = jnp.full(m_scr.shape, MASK_VALUE, F32)
    l_scr[...] = jnp.zeros(l_scr.shape, F32)
    acc_scr[...] = jnp.zeros(acc_scr.shape, F32)
    qk = D_NOPE + D_ROPE

    def update(kn, kpe, v, mask):
        ss = []
        for hd in range(H_MLA):
            kb = jnp.concatenate([kn[:, hd * D_NOPE:(hd + 1) * D_NOPE], kpe], axis=-1)
            ss.append(_dot_nt(kb, q_ref[:, hd * Q_HEAD_COLS:hd * Q_HEAD_COLS + qk]))
        for hd, s in enumerate(ss):
            if mask is not None:
                s = jnp.where(mask, s, MASK_VALUE)
            m_old = m_scr[hd]
            m_new = jnp.maximum(m_old, jnp.max(s, axis=0, keepdims=True))
            alpha = jnp.exp2(m_old - m_new)
            p = jnp.exp2(s - m_new)
            l_scr[hd] = alpha * l_scr[hd] + jnp.sum(p, axis=0, keepdims=True)
            acc_scr[hd] = alpha * acc_scr[hd] + _dot_tn(v[:, hd * D_VA:(hd + 1) * D_VA], p.astype(BF16))
            m_scr[hd] = m_new

    def body(j, carry):
        rows = pl.ds(pl.multiple_of(j * tk, tk), tk)
        update(knp_ref[rows, :], kpep_ref[rows, :].astype(BF16), vp_ref[rows, :], None)
        return carry

    lax.fori_loop(0, past // tk, body, 0)

    if t > CHUNK:
        kc = lax.broadcasted_iota(jnp.int32, (t, t), 0) // CHUNK
        qc = lax.broadcasted_iota(jnp.int32, (t, t), 1) // CHUNK
        mask = kc <= qc
    else:
        mask = None
    update(knn_ref[...], kpen_ref[:, 0:D_ROPE], vn_ref[...], mask)

    for hd in range(H_MLA):
        o = acc_scr[hd] * (1.0 / l_scr[hd])
        o_ref[:, hd * D_VA:(hd + 1) * D_VA] = o.T


def _mla_attn_cached(q, knp, kpe_cache, vp, knn, kpen, vn, *, layer, batch, t, past, tk):
    assert past % tk == 0 and past % CHUNK == 0 and t % CHUNK == 0
    vcols = H_MLA * D_VA
    new = lambda a: pl.BlockSpec((t, a.shape[1]), lambda b: (b, 0))
    old = lambda a: pl.BlockSpec((past, a.shape[1]), lambda b: (b, 0))
    scratch = [pltpu.VMEM((H_MLA, 1, t), F32), pltpu.VMEM((H_MLA, 1, t), F32), pltpu.VMEM((H_MLA, D_VA, t), F32)]
    return pl.pallas_call(
        functools.partial(_mla_cached_kernel, t=t, tk=tk, past=past),
        grid=(batch,),
        in_specs=[new(q), old(knp),
                  pl.BlockSpec((None, None, past, D_ROPE), lambda b: (layer, b, 0, 0)),
                  old(vp), new(knn), new(kpen), new(vn)],
        out_specs=pl.BlockSpec((t, vcols), lambda b: (b, 0)),
        out_shape=jax.ShapeDtypeStruct((batch * t, vcols), F32),
        scratch_shapes=scratch,
        compiler_params=_params("parallel"),
        name="mla_attn_cached",
    )(q, knp, kpe_cache, vp, knn, kpen, vn)


def _mla_first_kernel(q_ref, kn_ref, kpe_ref, vt_ref, o_ref, qt_scr, m_scr, acc_scr, s_scr, *, tq, tk):
    i = pl.program_id(1)
    nb = tq // tk
    assert nb == 2
    for hd in range(H_MLA):
        qt_scr[hd] = q_ref[:, hd * Q_HEAD_COLS:(hd + 1) * Q_HEAD_COLS].T
    m_scr[...] = jnp.full(m_scr.shape, MASK_VALUE, F32)
    acc_scr[...] = jnp.zeros(acc_scr.shape, F32)
    last = i * nb + nb - 1

    def scores(b, buf):
        r0 = pl.multiple_of(b * tk, tk)
        kpe = kpe_ref[pl.ds(r0, tk), :]
        for hd in range(H_MLA):
            kb = jnp.concatenate([kn_ref[pl.ds(r0, tk), hd * D_NOPE:(hd + 1) * D_NOPE], kpe], axis=-1)
            s = _dot(kb, qt_scr[hd])
            for g in range(tq // LANES):
                s_scr[buf, hd, g] = s[:, g * LANES:(g + 1) * LANES]

    def values(b, buf, own):
        r0 = pl.multiple_of(b * tk, tk)
        for hd in range(H_MLA):
            vt = vt_ref[hd * VT_ROWS:(hd + 1) * VT_ROWS, pl.ds(r0, tk)]
            for w0 in range(0, tq, MXU_COLS):
                ps, alphas = [], []
                for c0 in range(w0, w0 + MXU_COLS, LANES):
                    qs = slice(c0, c0 + LANES)
                    s = s_scr[buf, hd, c0 // LANES]
                    if own is not None:
                        kc = lax.broadcasted_iota(jnp.int32, (tk, LANES), 0) // CHUNK + own * (tk // CHUNK)
                        qc = (lax.broadcasted_iota(jnp.int32, (tk, LANES), 1) + c0) // CHUNK
                        s = jnp.where(kc <= qc, s, MASK_VALUE)
                    m_old = m_scr[hd, :, qs]
                    m_new = jnp.maximum(m_old, jnp.max(s, axis=0, keepdims=True))
                    m_scr[hd, :, qs] = m_new
                    alphas.append(jnp.exp2(m_old - m_new))
                    ps.append(jnp.exp2(s - m_new).astype(BF16))
                w = w0 // MXU_COLS
                acc_scr[hd, w] = (jnp.concatenate(alphas, axis=1) * acc_scr[hd, w]
                                  + _dot(vt, jnp.concatenate(ps, axis=1)))

    def step(t, buf, own=None):
        scores(t, buf)
        values(t - 1, 1 - buf, own)

    scores(0, 0)

    def pair(k, carry):
        t = 1 + 2 * k
        step(t, 1)
        step(t + 1, 0)
        return carry

    def two_pairs(k, carry):
        return pair(2 * k + 1, pair(2 * k, carry))

    lax.fori_loop(0, i // 2, two_pairs, 0)
    lax.fori_loop(i // 2 * 2, i, pair, 0)
    step(last, 1, own=0)
    values(last, 1, 1)

    for hd in range(H_MLA):
        for w in range(tq // MXU_COLS):
            o = acc_scr[hd, w, 0:D_VA, :] * (1.0 / acc_scr[hd, w, D_VA:D_VA + 1, :])
            o_ref[w * MXU_COLS:(w + 1) * MXU_COLS, hd * D_VA:(hd + 1) * D_VA] = o.T


def _mla_attn_first(q, kn, kpe, vt, *, batch, t, tq, tk):
    assert t % tq == 0 and tq == 2 * tk and tk % CHUNK == 0
    nq = t // tq
    vcols = H_MLA * D_VA
    resident = lambda a: pl.BlockSpec((t, a.shape[1]), lambda b, i: (b, 0), pipeline_mode=pl.Buffered(1))
    scratch = [pltpu.VMEM((H_MLA, Q_HEAD_COLS, tq), BF16),
               pltpu.VMEM((H_MLA, 1, tq), F32),
               pltpu.VMEM((H_MLA, tq // MXU_COLS, VT_ROWS, MXU_COLS), F32),
               pltpu.VMEM((2, H_MLA, tq // LANES, tk, LANES), F32)]
    return pl.pallas_call(
        functools.partial(_mla_first_kernel, tq=tq, tk=tk),
        grid=(batch, nq),
        in_specs=[pl.BlockSpec((tq, q.shape[1]), lambda b, i: (b * nq + i, 0)), resident(kn), resident(kpe),
                  pl.BlockSpec((None, H_MLA * VT_ROWS, t), lambda b, i: (b, 0, 0), pipeline_mode=pl.Buffered(1))],
        out_specs=pl.BlockSpec((tq, vcols), lambda b, i: (b * nq + i, 0)),
        out_shape=jax.ShapeDtypeStruct((batch * t, vcols), F32),
        scratch_shapes=scratch,
        compiler_params=_params("parallel", "arbitrary"),
        name="mla_attn_first",
    )(q, kn, kpe, vt)


def _sb_kernel(q_ref, *refs, tq, tk, past):
    if past:
        kp_ref, vp_ref, k_ref, v_ref, o_ref, c_scr, acc_scr = refs
    else:
        k_ref, v_ref, o_ref, c_scr, acc_scr = refs
    i = pl.program_id(1)
    c_scr[...] = jnp.zeros(c_scr.shape, F32)
    acc_scr[...] = jnp.zeros(acc_scr.shape, F32)
    heads = [slice(hd * D_SB, (hd + 1) * D_SB) for hd in range(H_SB)]

    def later_eq(n):
        return (lax.broadcasted_iota(jnp.int32, (n, n), 1)
                >= lax.broadcasted_iota(jnp.int32, (n, n), 0)).astype(BF16)

    def new_rows(r0, n):
        return ([k_ref[pl.ds(r0, n), cols] for cols in heads], [v_ref[pl.ds(r0, n), cols] for cols in heads])

    def cached_rows(r0, n):
        return ([kp_ref[pl.ds(r0, n), hd, :].astype(BF16) for hd in range(H_SB)],
                [vp_ref[pl.ds(r0, n), hd, :].astype(BF16) for hd in range(H_SB)])

    def update(kv, n, tri, before):
        ks, vs = kv
        zs = [_dot_nt(ks[hd], q_ref[:, cols]) for hd, cols in enumerate(heads)]
        es, incls = [], []
        for z in zs:
            ls = jnp.minimum(-z, 0.0) - jnp.log2(1.0 + jnp.exp2(-jnp.abs(z)))
            lsm = ls if before is None else jnp.where(before, ls, 0.0)
            incl = _dot(tri, lsm.astype(BF16))
            incls.append(incl)
            es.append(z + incl if before is None else z + ls + (incl - lsm))
        top = None
        for hd in range(H_SB):
            c = c_scr[hd]
            w = jnp.exp2(es[hd] + c)
            if before is not None:
                w = jnp.where(before, w, 0.0)
            acc_scr[hd] += _dot_tn(vs[hd], w.astype(BF16))
            c_new = c + incls[hd][0:1, :]
            c_scr[hd] = c_new
            top_h = jnp.max(c_new)
            top = top_h if top is None else jnp.maximum(top, top_h)
        return top

    own0 = i * tq
    before = lax.broadcasted_iota(jnp.int32, (tq, tq), 0) < lax.broadcasted_iota(jnp.int32, (tq, tq), 1)
    top = update(new_rows(pl.multiple_of(own0, tq), tq), tq, later_eq(tq), before)
    tri_k = later_eq(tk)
    older = cached_rows if past else new_rows

    def cond(st):
        j, top = st
        return jnp.logical_and(j >= 0, top > SB_EXP2_ZERO)

    def body(st):
        j, _ = st
        return j - 1, update(older(pl.multiple_of(j * tk, tk), tk), tk, tri_k, None)

    lax.while_loop(cond, body, ((past + own0) // tk - 1, top))
    for hd in range(H_SB):
        o_ref[:, hd * D_SB:(hd + 1) * D_SB] = acc_scr[hd].T


def _sb_attn(q, k, v, *, batch, t, tq, tk, past=0, k_cache=None, v_cache=None, layer=0):
    assert t % tq == 0 and past % tk == 0 and (tq == tk if past == 0 else t == tq)
    nq = t // tq
    cols = H_SB * D_SB
    q_spec = pl.BlockSpec((tq, cols), lambda b, i: (b * nq + i, 0))
    if past:
        cache = pl.BlockSpec((None, None, past, H_SB, D_SB), lambda b, i: (layer, b, 0, 0, 0))
        new = pl.BlockSpec((t, cols), lambda b, i: (b, 0))
        in_specs, operands = [q_spec, cache, cache, new, new], (q, k_cache, v_cache, k, v)
    else:
        resident = pl.BlockSpec((t, cols), lambda b, i: (b, 0), pipeline_mode=pl.Buffered(1))
        in_specs, operands = [q_spec, resident, resident], (q, k, v)
    return pl.pallas_call(
        functools.partial(_sb_kernel, tq=tq, tk=tk, past=past),
        grid=(batch, nq),
        in_specs=in_specs,
        out_specs=q_spec,
        out_shape=jax.ShapeDtypeStruct((batch * t, cols), F32),
        scratch_shapes=[pltpu.VMEM((H_SB, 1, tq), F32), pltpu.VMEM((H_SB, D_SB, tq), F32)],
        compiler_params=_params("parallel", "arbitrary"),
        name="sb_attn",
    )(*operands)


def _post_kernel(oa_ref, ob_ref, x_ref, mod_ref, goa_ref, gob_ref, gnf_ref, wout_ref, wg_ref, wu_ref, wd_ref,
                 wconv_ref, bconv_ref, cbuf_ref, y_ref, nconv_ref, x1_scr, h_scr, u_scr,
                 *, seg, nseg, d, per_batch):
    i = pl.program_id(0)
    na = oa_ref.shape[1]
    ma = _rms_rows(oa_ref[...], goa_ref[...]).astype(BF16)
    mb = _rms_rows(ob_ref[...], gob_ref[...]).astype(BF16)
    mix = _dot(ma, wout_ref[0:na, :]) + _dot(mb, wout_ref[na:, :])
    for s in range(nseg):
        rows = slice(s * seg, (s + 1) * seg)
        m = mod_ref[s]
        x1 = x_ref[rows, :] + m[:, 2 * d:3 * d] * mix[rows, :]
        x1_scr[rows, :] = x1
        h = _rms_rows(x1, gnf_ref[...]) * (1.0 + m[:, 4 * d:5 * d]) + m[:, 3 * d:4 * d]
        h_scr[rows, :] = h.astype(BF16)
    h = h_scr[...]

    if nseg == 1:
        first = i % per_batch == 0

        @pl.when(first)
        def _():
            u_scr[0, 8 - (CONV_W - 1):8, :] = cbuf_ref[0]

        @pl.when(jnp.logical_not(first))
        def _():
            u_scr[0, 0:8, :] = u_scr[0, seg:seg + 8, :]
    else:
        for s in range(nseg):
            u_scr[s, 8 - (CONV_W - 1):8, :] = cbuf_ref[s]

    ff = wg_ref.shape[1]
    bounds = [ff * c // FF_CHUNKS // MXU_COLS * MXU_COLS for c in range(FF_CHUNKS)] + [ff]
    chunks = [slice(bounds[c], bounds[c + 1]) for c in range(FF_CHUNKS)]
    us = [_dot(h, wg_ref[:, c]) for c in chunks]
    ups = [_dot(h, wu_ref[:, c]) for c in chunks]
    down = None
    for c, u, up in zip(chunks, us, ups):
        for s in range(nseg):
            u_scr[s, 8:8 + seg, c] = u[s * seg:(s + 1) * seg, :]
            nconv_ref[s, :, c] = u[(s + 1) * seg - (CONV_W - 1):(s + 1) * seg, :]
        parts = []
        for s in range(nseg):
            uc = bconv_ref[:, c]
            for tap in range(CONV_W):
                lo = 8 - (CONV_W - 1) + tap
                uc = uc + u_scr[s, lo:lo + seg, c] * wconv_ref[tap:tap + 1, c]
            parts.append(uc)
        uc = parts[0] if nseg == 1 else jnp.concatenate(parts, axis=0)
        act = (uc * jax.nn.sigmoid(uc) * up).astype(BF16)
        part = _dot(act, wd_ref[c, :])
        down = part if down is None else down + part
    for s in range(nseg):
        rows = slice(s * seg, (s + 1) * seg)
        y_ref[rows, :] = x1_scr[rows, :] + mod_ref[s][:, 5 * d:6 * d] * down[rows, :]


def _post(oa, ob, x2, mod3, goa, gob, gnf, wout, wg, wu, wd, wconv, bconv, cbuf, *, layer, batch, t, tm_max):
    r, d = x2.shape
    ff = wg.shape[-1]
    tm, seg, nseg = _row_tiling(batch, t, tm_max)
    per_batch = max(1, t // tm)
    full = lambda shape: pl.BlockSpec(shape, lambda i: (0,) * len(shape))
    rows = lambda w: pl.BlockSpec((tm, w), lambda i: (i, 0))
    if nseg == 1:
        state_spec = pl.BlockSpec((1, CONV_W - 1, ff), lambda i: (i // per_batch, 0, 0))
    else:
        state_spec = pl.BlockSpec((nseg, CONV_W - 1, ff), lambda i: (i, 0, 0))
    return pl.pallas_call(
        functools.partial(_post_kernel, seg=seg, nseg=nseg, d=d, per_batch=per_batch),
        grid=(r // tm,),
        in_specs=[rows(oa.shape[1]), rows(ob.shape[1]), rows(d), _mod_spec(t, tm, nseg, 6 * d),
                  full((1, oa.shape[1])), full((1, ob.shape[1])), full((1, d)),
                  _layer_spec(wout, layer), _layer_spec(wg, layer), _layer_spec(wu, layer),
                  _layer_spec(wd, layer),
                  full((CONV_W, ff)), full((1, ff)), state_spec],
        out_specs=[rows(d), state_spec],
        out_shape=(jax.ShapeDtypeStruct((r, d), F32),
                   jax.ShapeDtypeStruct((batch, CONV_W - 1, ff), F32)),
        scratch_shapes=[pltpu.VMEM((tm, d), F32), pltpu.VMEM((tm, d), BF16),
                        pltpu.VMEM((nseg, seg + 8, ff), F32)],
        compiler_params=_params("arbitrary"),
        name="post_ffn",
    )(oa, ob, x2, mod3, goa, gob, gnf, wout, wg, wu, wd, wconv, bconv, cbuf)


def _in_col_runs():
    qa = H_MLA * (D_NOPE + D_ROPE)
    half = D_ROPE // 2
    runs = []
    for h in range(H_MLA):
        base = h * (D_NOPE + D_ROPE)
        pe = base + D_NOPE
        runs += [(base, pe + D_ROPE), (pe + half, pe + D_ROPE), (pe, pe + half)]
    runs.append((qa, qa + D_LATENT))
    kp = qa + D_LATENT
    runs += [(kp, kp + D_ROPE), (kp + half, kp + D_ROPE), (kp, kp + half)]
    runs.append((kp + D_ROPE, kp + D_ROPE + 3 * H_SB * D_SB))
    return runs


def _win_kernel(w_ref, o_ref):
    w = w_ref[...]
    o_ref[...] = jnp.concatenate([w[:, a:b] for a, b in _in_col_runs()], axis=1).astype(o_ref.dtype)


def _permute_w_in(w_in):
    depth, d, n = w_in.shape
    x = w_in.reshape(depth * d, n)
    tm = 256
    out = pl.pallas_call(
        _win_kernel,
        grid=(depth * d // tm,),
        in_specs=[pl.BlockSpec((tm, n), lambda i: (i, 0))],
        out_specs=pl.BlockSpec((tm, C_END), lambda i: (i, 0)),
        out_shape=jax.ShapeDtypeStruct((depth * d, C_END), BF16),
        compiler_params=_params("parallel"),
        name="permute_w_in",
    )(x)
    return out.reshape(depth, d, C_END)


def _rope_gain(g):
    half = D_ROPE // 2
    return jnp.concatenate([g, g[..., half:], g[..., :half]], axis=-1)


def _rope_tables(pos):
    half = D_ROPE // 2
    freqs = ROPE_THETA ** (-jnp.arange(half, dtype=F32) / half)
    ang = pos.astype(F32)[:, None] * freqs[None, :]
    cos, sin = jnp.cos(ang), jnp.sin(ang)
    return jnp.tile(cos, (1, 4)), jnp.concatenate([-sin, sin, sin, -sin], axis=-1)


def kernel(x_prompt, x_sample, c_prompt, c_sample, cache_mla_latent, cache_mla_krope, cache_sb_k, cache_sb_v, state_ffn_conv, w_ada, b_ada, g_norm_mix, g_norm_ffn, w_in, g_kv_latent, g_q_nope, g_q_rope, g_k_nope, g_k_rope, w_uk, w_uv, g_out_mla, g_out_sb, w_out, w_gate, w_up, w_conv, b_conv, w_down):
    depth = w_in.shape[0]
    bp, tp, d = x_prompt.shape
    bs, ts, _ = x_sample.shape
    past = cache_mla_latent.shape[2]
    ff = w_gate.shape[2]
    sb = H_SB * D_SB

    nb = bp + bs
    nb_pad = -(-nb // 8) * 8
    c_all = jnp.concatenate([c_prompt, c_sample, jnp.zeros((nb_pad - nb, d), F32)], axis=0)
    mod = _ada_all(c_all, w_ada, b_ada)

    w_in_p = _permute_w_in(w_in)
    w_uk16, w_uv16, w_out16 = _to_bf16(w_uk), _to_bf16(w_uv), _to_bf16(w_out)
    w_uvt16 = jnp.swapaxes(w_uv16, 1, 2)
    w_gate16, w_up16, w_down16 = _to_bf16(w_gate), _to_bf16(w_up), _to_bf16(w_down)
    gqr, gkr = _rope_gain(g_q_rope), _rope_gain(g_k_rope)

    cos_p, sin_p = _rope_tables(jnp.arange(tp, dtype=jnp.int32))
    cos_p, sin_p = jnp.tile(cos_p, (bp, 1)), jnp.tile(sin_p, (bp, 1))
    cos_s, sin_s = _rope_tables(past + jnp.arange(ts, dtype=jnp.int32))
    cos_s, sin_s = jnp.tile(cos_s, (bs, 1)), jnp.tile(sin_s, (bs, 1))

    lat_cache_rows = cache_mla_latent.reshape(-1, D_LATENT)

    def layer(l, x2, mod3, cos, sin, batch, t, p, stacks):
        row = lambda a: a[l][None, :]
        qm, lat_st, kpe_st, kpe128, qb, kb_st, vb_st, kb16, vb16 = _inproj(
            x2, mod3, row(g_norm_mix), w_in_p, row(g_q_nope), row(gqr), row(g_kv_latent), row(gkr),
            cos, sin, batch=batch, t=t, tm_max=512, layer=l, depth=depth, stacks=stacks)
        new_lat = dict(row0=l * batch * t, n_rows=batch * t)
        lat_rows = lat_st.reshape(-1, D_LATENT)
        if p:
            cbuf = state_ffn_conv[l]
            knp, vp = _kvup(lat_cache_rows, w_uk16, w_uv16, row(g_k_nope), layer=l, tm_max=2048,
                            row0=l * batch * p, n_rows=batch * p)
            knn, vn = _kvup(lat_rows, w_uk16, w_uv16, row(g_k_nope), layer=l, tm_max=512, **new_lat)
            oa = _mla_attn_cached(qm, knp, cache_mla_krope, vp, knn, kpe128, vn,
                                  layer=l, batch=batch, t=t, past=p, tk=256)
            ob = _sb_attn(qb, kb16, vb16, batch=batch, t=t, tq=t, tk=256, past=p,
                          k_cache=cache_sb_k, v_cache=cache_sb_v, layer=l)
        else:
            cbuf = jnp.zeros((batch, CONV_W - 1, ff), F32)
            kn, vt = _kvup(lat_rows, w_uk16, w_uvt16, row(g_k_nope), layer=l, tm_max=2048, per_batch_t=t,
                           **new_lat)
            oa = _mla_attn_first(qm, kn, kpe128, vt, batch=batch, t=t, tq=512, tk=256)
            ob = _sb_attn(qb, kb16, vb16, batch=batch, t=t, tq=256, tk=256)
        y, nconv = _post(oa, ob, x2, mod3, row(g_out_mla), row(g_out_sb), row(g_norm_ffn), w_out16,
                         w_gate16, w_up16, w_down16, w_conv[l], row(b_conv), cbuf,
                         layer=l, batch=batch, t=t, tm_max=256)
        return y, (lat_st, kpe_st, kb_st, vb_st), nconv

    xp = x_prompt.reshape(bp * tp, d)
    xs = x_sample.reshape(bs * ts, d)
    st_p = st_s = None
    conv_p, conv_s = [], []
    for l in range(depth):
        mod_p = mod[l, :bp].reshape(bp, 1, 6 * d)
        mod_s = mod[l, bp:nb].reshape(bs, 1, 6 * d)
        xp, st_p, nc_p = layer(l, xp, mod_p, cos_p, sin_p, bp, tp, 0, st_p)
        xs, st_s, nc_s = layer(l, xs, mod_s, cos_s, sin_s, bs, ts, past, st_s)
        conv_p.append(nc_p)
        conv_s.append(nc_s)

    def step_outputs(stacks, convs, batch, t):
        lat_st, kpe_st, kb_st, vb_st = stacks
        return (lat_st.reshape(depth, batch, t, D_LATENT), kpe_st.reshape(depth, batch, t, D_ROPE),
                kb_st.reshape(depth, batch, t, H_SB, D_SB), vb_st.reshape(depth, batch, t, H_SB, D_SB),
                jnp.stack(convs))

    return ((xp.reshape(bp, tp, d), xs.reshape(bs, ts, d))
            + step_outputs(st_p, conv_p, bp, tp) + step_outputs(st_s, conv_s, bs, ts))
```

```python
import functools
import math

import jax
import jax.numpy as jnp
from jax import lax
from jax.experimental import pallas as pl
from jax.experimental.pallas import tpu as pltpu

F32 = jnp.float32
BF16 = jnp.bfloat16

H_MLA = 4
D_NOPE = 128
D_ROPE = 64
D_VA = 128
D_LATENT = 512
H_SB = 4
D_SB = 128
CONV_W = 3
CHUNK = 64
ROPE_THETA = 10000.0
EPS = 1e-6

LANES = 128
MXU_COLS = 256
Q_HEAD_COLS = 2 * LANES
MASK_VALUE = -1e30
FF_CHUNKS = 2
ONES_ROWS = 16
VT_ROWS = D_VA + ONES_ROWS
SB_EXP2_ZERO = -152.0
VMEM_LIMIT_BYTES = 56 * 1024 * 1024

MLA_Q_SCALE = math.log2(math.e) * (D_NOPE + D_ROPE) ** -0.5
SB_Q_SCALE = math.log2(math.e) * D_SB ** -0.5


def _rms_rows(x, g):
    return x * lax.rsqrt(jnp.mean(x * x, axis=-1, keepdims=True) + EPS) * g


def _dot(a, b):
    return jnp.dot(a, b, preferred_element_type=F32)


def _dot_nt(a, b):
    return lax.dot_general(a, b, (((1,), (1,)), ((), ())), preferred_element_type=F32)


def _dot_tn(a, b):
    return lax.dot_general(a, b, (((0,), (0,)), ((), ())), preferred_element_type=F32)


def _layer_spec(w, layer):
    return pl.BlockSpec((None,) + w.shape[1:], lambda *_: (layer, 0, 0), pipeline_mode=pl.Buffered(1))


def _params(*sem):
    return pltpu.CompilerParams(dimension_semantics=sem, vmem_limit_bytes=VMEM_LIMIT_BYTES)


def _cast_kernel(x_ref, o_ref):
    o_ref[...] = x_ref[...].astype(o_ref.dtype)


def _to_bf16(w):
    x = w.reshape(-1, w.shape[-1])
    r, c = x.shape
    tm = 512
    while r % tm:
        tm //= 2
    out = pl.pallas_call(
        _cast_kernel,
        grid=(r // tm,),
        in_specs=[pl.BlockSpec((tm, c), lambda i: (i, 0))],
        out_specs=pl.BlockSpec((tm, c), lambda i: (i, 0)),
        out_shape=jax.ShapeDtypeStruct((r, c), BF16),
        compiler_params=_params("parallel"),
        name="cast_bf16",
    )(x)
    return out.reshape(w.shape)


def _ada_kernel(c_ref, w_ref, b_ref, o_ref):
    c = c_ref[...]
    s = (c * jax.nn.sigmoid(c)).astype(BF16)
    o_ref[0] = _dot(s, w_ref[0].astype(BF16)) + b_ref[0]


def _ada_all(c_all, w_ada, b_ada):
    depth, d, n = w_ada.shape
    bp = c_all.shape[0]
    tn = d
    return pl.pallas_call(
        _ada_kernel,
        grid=(depth, n // tn),
        in_specs=[
            pl.BlockSpec((bp, d), lambda l, j: (0, 0)),
            pl.BlockSpec((1, d, tn), lambda l, j: (l, 0, j)),
            pl.BlockSpec((1, 1, tn), lambda l, j: (l, 0, j)),
        ],
        out_specs=pl.BlockSpec((1, bp, tn), lambda l, j: (l, 0, j)),
        out_shape=jax.ShapeDtypeStruct((depth, bp, n), F32),
        compiler_params=_params("parallel", "parallel"),
        name="ada_mod",
    )(c_all, w_ada, b_ada.reshape(depth, 1, n))


C_QM = 0
C_LAT = H_MLA * Q_HEAD_COLS
C_KPE = C_LAT + D_LATENT
C_QB = C_KPE + LANES
C_KB = C_QB + H_SB * D_SB
C_VB = C_KB + H_SB * D_SB
C_END = C_VB + H_SB * D_SB


def _rope_pair(v, g, cos, sin):
    ssq = jnp.sum(v * v, axis=-1, keepdims=True) * (0.5 / D_ROPE)
    vn = v * lax.rsqrt(ssq + EPS) * g
    return vn * cos + pltpu.roll(vn, D_ROPE, 1) * sin


N_INPROJ_IN = 10


def _inproj_kernel(*refs, seg, nseg, d):
    x_ref, mod_ref, gn_ref, w_ref, gqn_ref, gqr_ref, glat_ref, gkr_ref, cos_ref, sin_ref = refs[:N_INPROJ_IN]
    qm_ref, lat_ref, kpe_ref, kpe128_ref, qb_ref, kb_ref, vb_ref, kb16_ref, vb16_ref, h_scr = refs[-10:]
    for s in range(nseg):
        rows = slice(s * seg, (s + 1) * seg)
        m = mod_ref[s]
        h = _rms_rows(x_ref[rows, :], gn_ref[...]) * (1.0 + m[:, d:2 * d]) + m[:, 0:d]
        h_scr[rows, :] = h.astype(BF16)
    h = h_scr[...]
    cos = cos_ref[...]
    sin = sin_ref[...]

    for hd in range(H_MLA):
        c0 = C_QM + hd * Q_HEAD_COLS
        qh = _dot(h, w_ref[:, c0:c0 + Q_HEAD_COLS])
        qn = _rms_rows(qh[:, :D_NOPE], gqn_ref[...]) * MLA_Q_SCALE
        qp = _rope_pair(qh[:, D_NOPE:], gqr_ref[...], cos, sin) * MLA_Q_SCALE
        qm_ref[:, c0:c0 + D_NOPE] = qn.astype(BF16)
        qm_ref[:, c0 + D_NOPE:c0 + Q_HEAD_COLS] = qp.astype(BF16)

    lat_ref[...] = _rms_rows(_dot(h, w_ref[:, C_LAT:C_KPE]), glat_ref[...])

    kp = _rope_pair(_dot(h, w_ref[:, C_KPE:C_QB]), gkr_ref[...], cos, sin)
    kpe_ref[...] = kp[:, :D_ROPE]
    lane = lax.broadcasted_iota(jnp.int32, kp.shape, 1)
    kpe128_ref[...] = jnp.where(lane < D_ROPE, kp, 0.0).astype(BF16)

    qb_ref[...] = (_dot(h, w_ref[:, C_QB:C_KB]) * SB_Q_SCALE).astype(BF16)
    n = h.shape[0]
    kb = _dot(h, w_ref[:, C_KB:C_VB])
    kb16_ref[...] = kb.astype(BF16)
    vb = _dot(h, w_ref[:, C_VB:C_END])
    vb16_ref[...] = vb.astype(BF16)
    for hd in range(H_SB):
        head_rows = pl.ds(hd, n, stride=H_SB)
        kb_ref[head_rows, :] = kb[:, hd * D_SB:(hd + 1) * D_SB]
        vb_ref[head_rows, :] = vb[:, hd * D_SB:(hd + 1) * D_SB]


def _row_tiling(batch, t, tm_max):
    if t >= tm_max:
        assert t % tm_max == 0
        return tm_max, tm_max, 1
    nseg = max(1, min(batch, tm_max // t))
    while batch % nseg:
        nseg -= 1
    return nseg * t, t, nseg


def _mod_spec(t, tm, nseg, width):
    if nseg == 1:
        per = t // tm
        return pl.BlockSpec((1, 1, width), lambda i: (i // per, 0, 0))
    return pl.BlockSpec((nseg, 1, width), lambda i: (i, 0, 0))


STACKED_OUTS = (1, 2, 5, 6)


def _inproj(x2, mod3, gn, w_p, gqn, gqr, glat, gkr, cos, sin, *, batch, t, tm_max, layer, depth, stacks):
    r, d = x2.shape
    tm, seg, nseg = _row_tiling(batch, t, tm_max)
    full = lambda shape: pl.BlockSpec(shape, lambda i: (0,) * len(shape))
    rows = lambda w: pl.BlockSpec((tm, w), lambda i: (i, 0))
    sb = H_SB * D_SB
    outs = ((1, H_MLA * Q_HEAD_COLS, BF16),
            (1, D_LATENT, F32),
            (1, D_ROPE, F32),
            (1, LANES, BF16),
            (1, sb, BF16),
            (H_SB, D_SB, F32), (H_SB, D_SB, F32),
            (1, sb, BF16), (1, sb, BF16))
    out_shapes, out_specs = [], []
    for k, (per, w, dt) in enumerate(outs):
        if k in STACKED_OUTS:
            out_shapes.append(jax.ShapeDtypeStruct((depth, r * per, w), dt))
            out_specs.append(pl.BlockSpec((None, tm * per, w), lambda i: (layer, i, 0)))
        else:
            out_shapes.append(jax.ShapeDtypeStruct((r, w), dt))
            out_specs.append(rows(w))
    in_specs = [rows(d), _mod_spec(t, tm, nseg, 6 * d), full((1, d)), _layer_spec(w_p, layer),
                full((1, D_NOPE)), full((1, LANES)), full((1, D_LATENT)), full((1, LANES)),
                rows(LANES), rows(LANES)]
    operands = (x2, mod3, gn, w_p, gqn, gqr, glat, gkr, cos, sin)
    assert len(operands) == N_INPROJ_IN
    aliases = {}
    if stacks is not None:
        in_specs += [pl.BlockSpec(memory_space=pl.ANY)] * len(STACKED_OUTS)
        operands += tuple(stacks)
        aliases = {N_INPROJ_IN + n: k for n, k in enumerate(STACKED_OUTS)}
    return pl.pallas_call(
        functools.partial(_inproj_kernel, seg=seg, nseg=nseg, d=d),
        grid=(r // tm,),
        in_specs=in_specs,
        out_specs=out_specs,
        out_shape=tuple(out_shapes),
        input_output_aliases=aliases,
        scratch_shapes=[pltpu.VMEM((tm, d), BF16)],
        compiler_params=_params("parallel"),
        name="in_proj",
    )(*operands)


def _kvup_kernel(lat_ref, wuk_ref, wuv_ref, gkn_ref, k_ref, v_ref, *, v_transposed):
    l16 = lat_ref[...].astype(BF16)
    kn = _dot(l16, wuk_ref[...])
    for hd in range(H_MLA):
        cols = slice(hd * D_NOPE, (hd + 1) * D_NOPE)
        k_ref[:, cols] = _rms_rows(kn[:, cols], gkn_ref[...]).astype(BF16)
    if v_transposed:
        vt = _dot_nt(wuv_ref[...], l16).astype(BF16)
        ones = jnp.ones((ONES_ROWS, vt.shape[1]), BF16)
        for hd in range(H_MLA):
            v_ref[hd * VT_ROWS:hd * VT_ROWS + D_VA, :] = vt[hd * D_VA:(hd + 1) * D_VA, :]
            v_ref[hd * VT_ROWS + D_VA:(hd + 1) * VT_ROWS, :] = ones
    else:
        v_ref[...] = _dot(l16, wuv_ref[...]).astype(BF16)


def _kvup(lat_all, wuk, wuv, gkn, *, layer, tm_max, per_batch_t=None, row0=0, n_rows=None):
    r = lat_all.shape[0] if n_rows is None else n_rows
    tm = min(tm_max, r if per_batch_t is None else per_batch_t)
    while r % tm or row0 % tm:
        tm //= 2
    off = row0 // tm
    full = lambda shape: pl.BlockSpec(shape, lambda i: (0,) * len(shape))
    rows = lambda w: pl.BlockSpec((tm, w), lambda i: (i, 0))
    vcols = H_MLA * D_VA
    if per_batch_t is None:
        v_spec, v_shape = rows(vcols), jax.ShapeDtypeStruct((r, vcols), BF16)
    else:
        per = per_batch_t // tm
        v_spec = pl.BlockSpec((None, H_MLA * VT_ROWS, tm), lambda i: (i // per, 0, i % per))
        v_shape = jax.ShapeDtypeStruct((r // per_batch_t, H_MLA * VT_ROWS, per_batch_t), BF16)
    return pl.pallas_call(
        functools.partial(_kvup_kernel, v_transposed=per_batch_t is not None),
        grid=(r // tm,),
        in_specs=[pl.BlockSpec((tm, D_LATENT), lambda i: (i + off, 0)),
                  _layer_spec(wuk, layer), _layer_spec(wuv, layer), full((1, D_NOPE))],
        out_specs=[rows(H_MLA * D_NOPE), v_spec],
        out_shape=(jax.ShapeDtypeStruct((r, H_MLA * D_NOPE), BF16), v_shape),
        compiler_params=_params("parallel"),
        name="kv_up",
    )(lat_all, wuk, wuv, gkn)


def _mla_cached_kernel(q_ref, knp_ref, kpep_ref, vp_ref, knn_ref, kpen_ref, vn_ref, o_ref,
                       m_scr, l_scr, acc_scr, *, t, tk, past):
    m_scr[...] = jnp.full(m_scr.shape, MASK_VALUE, F32)
    l_scr[...] = jnp.zeros(l_scr.shape, F32)
    acc_scr[...] = jnp.zeros(acc_scr.shape, F32)
    qk = D_NOPE + D_ROPE

    def scores(kn, kpe):
        ss = []
        for hd in range(H_MLA):
            kb = jnp.concatenate([kn[:, hd * D_NOPE:(hd + 1) * D_NOPE], kpe], axis=-1)
            ss.append(_dot_nt(kb, q_ref[:, hd * Q_HEAD_COLS:hd * Q_HEAD_COLS + qk]))
        return ss

    def values(ss, v, mask):
        for hd, s in enumerate(ss):
            if mask is not None:
                s = jnp.where(mask, s, MASK_VALUE)
            m_old = m_scr[hd]
            m_new = jnp.maximum(m_old, jnp.max(s, axis=0, keepdims=True))
            alpha = jnp.exp2(m_old - m_new)
            p = jnp.exp2(s - m_new)
            l_scr[hd] = alpha * l_scr[hd] + jnp.sum(p, axis=0, keepdims=True)
            acc_scr[hd] = alpha * acc_scr[hd] + _dot_tn(v[:, hd * D_VA:(hd + 1) * D_VA], p.astype(BF16))
            m_scr[hd] = m_new

    if t > CHUNK:
        kc = lax.broadcasted_iota(jnp.int32, (t, t), 0) // CHUNK
        qc = lax.broadcasted_iota(jnp.int32, (t, t), 1) // CHUNK
        mask = kc <= qc
    else:
        mask = None

    def cached(j):
        return slice(j * tk, (j + 1) * tk)

    nblk = past // tk
    ss = scores(knp_ref[cached(0), :], kpep_ref[cached(0), :].astype(BF16)) if nblk else None
    for j in range(nblk):
        if j + 1 < nblk:
            ss_next = scores(knp_ref[cached(j + 1), :], kpep_ref[cached(j + 1), :].astype(BF16))
        else:
            ss_next = scores(knn_ref[...], kpen_ref[:, 0:D_ROPE])
        values(ss, vp_ref[cached(j), :], None)
        ss = ss_next
    if not nblk:
        ss = scores(knn_ref[...], kpen_ref[:, 0:D_ROPE])
    values(ss, vn_ref[...], mask)

    for hd in range(H_MLA):
        o = acc_scr[hd] * (1.0 / l_scr[hd])
        o_ref[:, hd * D_VA:(hd + 1) * D_VA] = o.T


def _mla_attn_cached(q, knp, kpe_cache, vp, knn, kpen, vn, *, layer, batch, t, past, tk):
    assert past % tk == 0 and past % CHUNK == 0 and t % CHUNK == 0
    vcols = H_MLA * D_VA
    new = lambda a: pl.BlockSpec((t, a.shape[1]), lambda b: (b, 0))
    old = lambda a: pl.BlockSpec((past, a.shape[1]), lambda b: (b, 0))
    scratch = [pltpu.VMEM((H_MLA, 1, t), F32), pltpu.VMEM((H_MLA, 1, t), F32), pltpu.VMEM((H_MLA, D_VA, t), F32)]
    return pl.pallas_call(
        functools.partial(_mla_cached_kernel, t=t, tk=tk, past=past),
        grid=(batch,),
        in_specs=[new(q), old(knp),
                  pl.BlockSpec((None, None, past, D_ROPE), lambda b: (layer, b, 0, 0)),
                  old(vp), new(knn), new(kpen), new(vn)],
        out_specs=pl.BlockSpec((t, vcols), lambda b: (b, 0)),
        out_shape=jax.ShapeDtypeStruct((batch * t, vcols), F32),
        scratch_shapes=scratch,
        compiler_params=_params("parallel"),
        name="mla_attn_cached",
    )(q, knp, kpe_cache, vp, knn, kpen, vn)


def _mla_first_kernel(q_ref, kn_ref, kpe_ref, vt_ref, o_ref, qt_scr, m_scr, acc_scr, s_scr, *, tq, tk):
    i = pl.program_id(1)
    nb = tq // tk
    assert nb == 2
    for hd in range(H_MLA):
        qt_scr[hd] = q_ref[:, hd * Q_HEAD_COLS:(hd + 1) * Q_HEAD_COLS].T
    m_scr[...] = jnp.full(m_scr.shape, MASK_VALUE, F32)
    acc_scr[...] = jnp.zeros(acc_scr.shape, F32)
    last = i * nb + nb - 1

    def scores(b, buf):
        r0 = pl.multiple_of(b * tk, tk)
        kpe = kpe_ref[pl.ds(r0, tk), :]
        for hd in range(H_MLA):
            kb = jnp.concatenate([kn_ref[pl.ds(r0, tk), hd * D_NOPE:(hd + 1) * D_NOPE], kpe], axis=-1)
            s = _dot(kb, qt_scr[hd])
            for g in range(tq // LANES):
                s_scr[buf, hd, g] = s[:, g * LANES:(g + 1) * LANES]

    def values(b, buf, own):
        r0 = pl.multiple_of(b * tk, tk)
        for hd in range(H_MLA):
            vt = vt_ref[hd * VT_ROWS:(hd + 1) * VT_ROWS, pl.ds(r0, tk)]
            for w0 in range(0, tq, MXU_COLS):
                ps, alphas = [], []
                for c0 in range(w0, w0 + MXU_COLS, LANES):
                    qs = slice(c0, c0 + LANES)
                    s = s_scr[buf, hd, c0 // LANES]
                    if own is not None:
                        kc = lax.broadcasted_iota(jnp.int32, (tk, LANES), 0) // CHUNK + own * (tk // CHUNK)
                        qc = (lax.broadcasted_iota(jnp.int32, (tk, LANES), 1) + c0) // CHUNK
                        s = jnp.where(kc <= qc, s, MASK_VALUE)
                    m_old = m_scr[hd, :, qs]
                    m_new = jnp.maximum(m_old, jnp.max(s, axis=0, keepdims=True))
                    m_scr[hd, :, qs] = m_new
                    alphas.append(jnp.exp2(m_old - m_new))
                    ps.append(jnp.exp2(s - m_new).astype(BF16))
                w = w0 // MXU_COLS
                acc_scr[hd, w] = (jnp.concatenate(alphas, axis=1) * acc_scr[hd, w]
                                  + _dot(vt, jnp.concatenate(ps, axis=1)))

    def step(t, buf, own=None):
        scores(t, buf)
        values(t - 1, 1 - buf, own)

    scores(0, 0)

    def pair(k, carry):
        t = 1 + 2 * k
        step(t, 1)
        step(t + 1, 0)
        return carry

    def two_pairs(k, carry):
        return pair(2 * k + 1, pair(2 * k, carry))

    lax.fori_loop(0, i // 2, two_pairs, 0)
    lax.fori_loop(i // 2 * 2, i, pair, 0)
    step(last, 1, own=0)
    values(last, 1, 1)

    for hd in range(H_MLA):
        for w in range(tq // MXU_COLS):
            o = acc_scr[hd, w, 0:D_VA, :] * (1.0 / acc_scr[hd, w, D_VA:D_VA + 1, :])
            o_ref[w * MXU_COLS:(w + 1) * MXU_COLS, hd * D_VA:(hd + 1) * D_VA] = o.T


def _mla_attn_first(q, kn, kpe, vt, *, batch, t, tq, tk):
    assert t % tq == 0 and tq == 2 * tk and tk % CHUNK == 0
    nq = t // tq
    vcols = H_MLA * D_VA
    resident = lambda a: pl.BlockSpec((t, a.shape[1]), lambda b, i: (b, 0), pipeline_mode=pl.Buffered(1))
    scratch = [pltpu.VMEM((H_MLA, Q_HEAD_COLS, tq), BF16),
               pltpu.VMEM((H_MLA, 1, tq), F32),
               pltpu.VMEM((H_MLA, tq // MXU_COLS, VT_ROWS, MXU_COLS), F32),
               pltpu.VMEM((2, H_MLA, tq // LANES, tk, LANES), F32)]
    return pl.pallas_call(
        functools.partial(_mla_first_kernel, tq=tq, tk=tk),
        grid=(batch, nq),
        in_specs=[pl.BlockSpec((tq, q.shape[1]), lambda b, i: (b * nq + i, 0)), resident(kn), resident(kpe),
                  pl.BlockSpec((None, H_MLA * VT_ROWS, t), lambda b, i: (b, 0, 0), pipeline_mode=pl.Buffered(1))],
        out_specs=pl.BlockSpec((tq, vcols), lambda b, i: (b * nq + i, 0)),
        out_shape=jax.ShapeDtypeStruct((batch * t, vcols), F32),
        scratch_shapes=scratch,
        compiler_params=_params("parallel", "arbitrary"),
        name="mla_attn_first",
    )(q, kn, kpe, vt)


def _sb_kernel(q_ref, *refs, tq, tsub, tk, past):
    if past:
        kp_ref, vp_ref, k_ref, v_ref, o_ref, c_scr, acc_scr = refs
    else:
        k_ref, v_ref, o_ref, c_scr, acc_scr = refs
    i = pl.program_id(1)
    nsub = tq // tsub
    c_scr[...] = jnp.zeros(c_scr.shape, F32)
    acc_scr[...] = jnp.zeros(acc_scr.shape, F32)
    heads = [slice(hd * D_SB, (hd + 1) * D_SB) for hd in range(H_SB)]
    chains = [(u, hd) for u in range(nsub) for hd in range(H_SB)]

    def later_eq(n):
        return (lax.broadcasted_iota(jnp.int32, (n, n), 1)
                >= lax.broadcasted_iota(jnp.int32, (n, n), 0)).astype(BF16)

    def new_rows(r0, n):
        return ([k_ref[pl.ds(r0, n), cols] for cols in heads], [v_ref[pl.ds(r0, n), cols] for cols in heads])

    def cached_rows(r0, n):
        return ([kp_ref[pl.ds(r0, n), hd, :].astype(BF16) for hd in range(H_SB)],
                [vp_ref[pl.ds(r0, n), hd, :].astype(BF16) for hd in range(H_SB)])

    def update(kvs, live, tri, before):
        zs = [_dot_nt(kvs[u][0][hd], q_ref[u * tsub:(u + 1) * tsub, heads[hd]]) for u, hd in chains]
        es, incls = [], []
        for z in zs:
            ls = jnp.minimum(-z, 0.0) - jnp.log2(1.0 + jnp.exp2(-jnp.abs(z)))
            lsm = ls if before is None else jnp.where(before, ls, 0.0)
            incl = _dot(tri, lsm.astype(BF16))
            incls.append(incl)
            es.append(z + incl if before is None else z + ls + (incl - lsm))
        top = None
        for n, (u, hd) in enumerate(chains):
            c = c_scr[u, hd]
            e = es[n] + c
            if live[u] is not None:
                e = jnp.where(live[u], e, MASK_VALUE)
            w = jnp.exp2(e)
            if before is not None:
                w = jnp.where(before, w, 0.0)
            acc_scr[u, hd] += _dot_tn(kvs[u][1][hd], w.astype(BF16))
            c_new = c + incls[n][0:1, :]
            c_scr[u, hd] = c_new
            top_c = jnp.max(c_new)
            if live[u] is not None:
                top_c = jnp.where(live[u], top_c, MASK_VALUE)
            top = top_c if top is None else jnp.maximum(top, top_c)
        return top

    own0 = i * tq
    before = (lax.broadcasted_iota(jnp.int32, (tsub, tsub), 0)
              < lax.broadcasted_iota(jnp.int32, (tsub, tsub), 1))
    top = update([new_rows(pl.multiple_of(own0 + u * tsub, tsub), tsub) for u in range(nsub)],
                 [None] * nsub, later_eq(tsub), before)
    tri_k = later_eq(tk)
    older = cached_rows if past else new_rows
    newest = [(past + own0 + u * tsub) // tk - 1 for u in range(nsub)]

    def cond(st):
        k, top = st
        return jnp.logical_and(newest[-1] - k >= 0, top > SB_EXP2_ZERO)

    def body(st):
        k, _ = st
        kvs, live = [], []
        for u in range(nsub):
            j = newest[u] - k
            kvs.append(older(pl.multiple_of(jnp.maximum(j, 0) * tk, tk), tk))
            live.append(None if u == nsub - 1 else j >= 0)
        return k + 1, update(kvs, live, tri_k, None)

    lax.while_loop(cond, body, (0, top))
    for u, hd in chains:
        o_ref[u * tsub:(u + 1) * tsub, heads[hd]] = acc_scr[u, hd].T


def _sb_attn(q, k, v, *, batch, t, tq, tk, past=0, k_cache=None, v_cache=None, layer=0):
    tsub = tq if past else tk
    assert t % tq == 0 and past % tk == 0 and tq % tsub == 0 and (past == 0 or t == tq)
    nq = t // tq
    nsub = tq // tsub
    cols = H_SB * D_SB
    q_spec = pl.BlockSpec((tq, cols), lambda b, i: (b * nq + i, 0))
    if past:
        cache = pl.BlockSpec((None, None, past, H_SB, D_SB), lambda b, i: (layer, b, 0, 0, 0))
        new = pl.BlockSpec((t, cols), lambda b, i: (b, 0))
        in_specs, operands = [q_spec, cache, cache, new, new], (q, k_cache, v_cache, k, v)
    else:
        resident = pl.BlockSpec((t, cols), lambda b, i: (b, 0), pipeline_mode=pl.Buffered(1))
        in_specs, operands = [q_spec, resident, resident], (q, k, v)
    return pl.pallas_call(
        functools.partial(_sb_kernel, tq=tq, tsub=tsub, tk=tk, past=past),
        grid=(batch, nq),
        in_specs=in_specs,
        out_specs=q_spec,
        out_shape=jax.ShapeDtypeStruct((batch * t, cols), F32),
        scratch_shapes=[pltpu.VMEM((nsub, H_SB, 1, tsub), F32), pltpu.VMEM((nsub, H_SB, D_SB, tsub), F32)],
        compiler_params=_params("parallel", "arbitrary"),
        name="sb_attn",
    )(*operands)


def _post_kernel(oa_ref, ob_ref, x_ref, mod_ref, goa_ref, gob_ref, gnf_ref, wout_ref, wg_ref, wu_ref, wd_ref,
                 wconv_ref, bconv_ref, cbuf_ref, y_ref, nconv_ref, x1_scr, h_scr, u_scr,
                 *, seg, nseg, d, per_batch):
    i = pl.program_id(0)
    na = oa_ref.shape[1]
    ma = _rms_rows(oa_ref[...], goa_ref[...]).astype(BF16)
    mb = _rms_rows(ob_ref[...], gob_ref[...]).astype(BF16)
    mix = _dot(ma, wout_ref[0:na, :]) + _dot(mb, wout_ref[na:, :])
    for s in range(nseg):
        rows = slice(s * seg, (s + 1) * seg)
        m = mod_ref[s]
        x1 = x_ref[rows, :] + m[:, 2 * d:3 * d] * mix[rows, :]
        x1_scr[rows, :] = x1
        h = _rms_rows(x1, gnf_ref[...]) * (1.0 + m[:, 4 * d:5 * d]) + m[:, 3 * d:4 * d]
        h_scr[rows, :] = h.astype(BF16)
    h = h_scr[...]

    if nseg == 1:
        first = i % per_batch == 0

        @pl.when(first)
        def _():
            u_scr[0, 8 - (CONV_W - 1):8, :] = cbuf_ref[0]

        @pl.when(jnp.logical_not(first))
        def _():
            u_scr[0, 0:8, :] = u_scr[0, seg:seg + 8, :]
    else:
        for s in range(nseg):
            u_scr[s, 8 - (CONV_W - 1):8, :] = cbuf_ref[s]

    ff = wg_ref.shape[1]
    bounds = [ff * c // FF_CHUNKS // MXU_COLS * MXU_COLS for c in range(FF_CHUNKS)] + [ff]
    chunks = [slice(bounds[c], bounds[c + 1]) for c in range(FF_CHUNKS)]
    us = [_dot(h, wg_ref[:, c]) for c in chunks]
    ups = [_dot(h, wu_ref[:, c]) for c in chunks]
    down = None
    for c, u, up in zip(chunks, us, ups):
        for s in range(nseg):
            u_scr[s, 8:8 + seg, c] = u[s * seg:(s + 1) * seg, :]
            nconv_ref[s, :, c] = u[(s + 1) * seg - (CONV_W - 1):(s + 1) * seg, :]
        parts = []
        for s in range(nseg):
            uc = bconv_ref[:, c]
            for tap in range(CONV_W):
                lo = 8 - (CONV_W - 1) + tap
                uc = uc + u_scr[s, lo:lo + seg, c] * wconv_ref[tap:tap + 1, c]
            parts.append(uc)
        uc = parts[0] if nseg == 1 else jnp.concatenate(parts, axis=0)
        act = (uc * jax.nn.sigmoid(uc) * up).astype(BF16)
        part = _dot(act, wd_ref[c, :])
        down = part if down is None else down + part
    for s in range(nseg):
        rows = slice(s * seg, (s + 1) * seg)
        y_ref[rows, :] = x1_scr[rows, :] + mod_ref[s][:, 5 * d:6 * d] * down[rows, :]


def _post(oa, ob, x2, mod3, goa, gob, gnf, wout, wg, wu, wd, wconv, bconv, cbuf, *, layer, batch, t, tm_max):
    r, d = x2.shape
    ff = wg.shape[-1]
    tm, seg, nseg = _row_tiling(batch, t, tm_max)
    per_batch = max(1, t // tm)
    full = lambda shape: pl.BlockSpec(shape, lambda i: (0,) * len(shape))
    rows = lambda w: pl.BlockSpec((tm, w), lambda i: (i, 0))
    if nseg == 1:
        state_spec = pl.BlockSpec((1, CONV_W - 1, ff), lambda i: (i // per_batch, 0, 0))
    else:
        state_spec = pl.BlockSpec((nseg, CONV_W - 1, ff), lambda i: (i, 0, 0))
    return pl.pallas_call(
        functools.partial(_post_kernel, seg=seg, nseg=nseg, d=d, per_batch=per_batch),
        grid=(r // tm,),
        in_specs=[rows(oa.shape[1]), rows(ob.shape[1]), rows(d), _mod_spec(t, tm, nseg, 6 * d),
                  full((1, oa.shape[1])), full((1, ob.shape[1])), full((1, d)),
                  _layer_spec(wout, layer), _layer_spec(wg, layer), _layer_spec(wu, layer),
                  _layer_spec(wd, layer),
                  full((CONV_W, ff)), full((1, ff)), state_spec],
        out_specs=[rows(d), state_spec],
        out_shape=(jax.ShapeDtypeStruct((r, d), F32),
                   jax.ShapeDtypeStruct((batch, CONV_W - 1, ff), F32)),
        scratch_shapes=[pltpu.VMEM((tm, d), F32), pltpu.VMEM((tm, d), BF16),
                        pltpu.VMEM((nseg, seg + 8, ff), F32)],
        compiler_params=_params("arbitrary"),
        name="post_ffn",
    )(oa, ob, x2, mod3, goa, gob, gnf, wout, wg, wu, wd, wconv, bconv, cbuf)


def _in_col_runs():
    qa = H_MLA * (D_NOPE + D_ROPE)
    half = D_ROPE // 2
    runs = []
    for h in range(H_MLA):
        base = h * (D_NOPE + D_ROPE)
        pe = base + D_NOPE
        runs += [(base, pe + D_ROPE), (pe + half, pe + D_ROPE), (pe, pe + half)]
    runs.append((qa, qa + D_LATENT))
    kp = qa + D_LATENT
    runs += [(kp, kp + D_ROPE), (kp + half, kp + D_ROPE), (kp, kp + half)]
    runs.append((kp + D_ROPE, kp + D_ROPE + 3 * H_SB * D_SB))
    return runs


def _win_kernel(w_ref, o_ref):
    w = w_ref[...]
    o_ref[...] = jnp.concatenate([w[:, a:b] for a, b in _in_col_runs()], axis=1).astype(o_ref.dtype)


def _permute_w_in(w_in):
    depth, d, n = w_in.shape
    x = w_in.reshape(depth * d, n)
    tm = 256
    out = pl.pallas_call(
        _win_kernel,
        grid=(depth * d // tm,),
        in_specs=[pl.BlockSpec((tm, n), lambda i: (i, 0))],
        out_specs=pl.BlockSpec((tm, C_END), lambda i: (i, 0)),
        out_shape=jax.ShapeDtypeStruct((depth * d, C_END), BF16),
        compiler_params=_params("parallel"),
        name="permute_w_in",
    )(x)
    return out.reshape(depth, d, C_END)


def _rope_gain(g):
    half = D_ROPE // 2
    return jnp.concatenate([g, g[..., half:], g[..., :half]], axis=-1)


def _rope_tables(pos):
    half = D_ROPE // 2
    freqs = ROPE_THETA ** (-jnp.arange(half, dtype=F32) / half)
    ang = pos.astype(F32)[:, None] * freqs[None, :]
    cos, sin = jnp.cos(ang), jnp.sin(ang)
    return jnp.tile(cos, (1, 4)), jnp.concatenate([-sin, sin, sin, -sin], axis=-1)


def kernel(x_prompt, x_sample, c_prompt, c_sample, cache_mla_latent, cache_mla_krope, cache_sb_k, cache_sb_v, state_ffn_conv, w_ada, b_ada, g_norm_mix, g_norm_ffn, w_in, g_kv_latent, g_q_nope, g_q_rope, g_k_nope, g_k_rope, w_uk, w_uv, g_out_mla, g_out_sb, w_out, w_gate, w_up, w_conv, b_conv, w_down):
    depth = w_in.shape[0]
    bp, tp, d = x_prompt.shape
    bs, ts, _ = x_sample.shape
    past = cache_mla_latent.shape[2]
    ff = w_gate.shape[2]
    sb = H_SB * D_SB

    nb = bp + bs
    nb_pad = -(-nb // 8) * 8
    c_all = jnp.concatenate([c_prompt, c_sample, jnp.zeros((nb_pad - nb, d), F32)], axis=0)
    mod = _ada_all(c_all, w_ada, b_ada)

    w_in_p = _permute_w_in(w_in)
    w_uk16, w_uv16, w_out16 = _to_bf16(w_uk), _to_bf16(w_uv), _to_bf16(w_out)
    w_uvt16 = jnp.swapaxes(w_uv16, 1, 2)
    w_gate16, w_up16, w_down16 = _to_bf16(w_gate), _to_bf16(w_up), _to_bf16(w_down)
    gqr, gkr = _rope_gain(g_q_rope), _rope_gain(g_k_rope)

    cos_p, sin_p = _rope_tables(jnp.arange(tp, dtype=jnp.int32))
    cos_p, sin_p = jnp.tile(cos_p, (bp, 1)), jnp.tile(sin_p, (bp, 1))
    cos_s, sin_s = _rope_tables(past + jnp.arange(ts, dtype=jnp.int32))
    cos_s, sin_s = jnp.tile(cos_s, (bs, 1)), jnp.tile(sin_s, (bs, 1))

    lat_cache_rows = cache_mla_latent.reshape(-1, D_LATENT)

    def layer(l, x2, mod3, cos, sin, batch, t, p, stacks):
        row = lambda a: a[l][None, :]
        qm, lat_st, kpe_st, kpe128, qb, kb_st, vb_st, kb16, vb16 = _inproj(
            x2, mod3, row(g_norm_mix), w_in_p, row(g_q_nope), row(gqr), row(g_kv_latent), row(gkr),
            cos, sin, batch=batch, t=t, tm_max=512, layer=l, depth=depth, stacks=stacks)
        new_lat = dict(row0=l * batch * t, n_rows=batch * t)
        lat_rows = lat_st.reshape(-1, D_LATENT)
        if p:
            cbuf = state_ffn_conv[l]
            knp, vp = _kvup(lat_cache_rows, w_uk16, w_uv16, row(g_k_nope), layer=l, tm_max=2048,
                            row0=l * batch * p, n_rows=batch * p)
            knn, vn = _kvup(lat_rows, w_uk16, w_uv16, row(g_k_nope), layer=l, tm_max=512, **new_lat)
            oa = _mla_attn_cached(qm, knp, cache_mla_krope, vp, knn, kpe128, vn,
                                  layer=l, batch=batch, t=t, past=p, tk=256)
            ob = _sb_attn(qb, kb16, vb16, batch=batch, t=t, tq=t, tk=256, past=p,
                          k_cache=cache_sb_k, v_cache=cache_sb_v, layer=l)
        else:
            cbuf = jnp.zeros((batch, CONV_W - 1, ff), F32)
            kn, vt = _kvup(lat_rows, w_uk16, w_uvt16, row(g_k_nope), layer=l, tm_max=2048, per_batch_t=t,
                           **new_lat)
            oa = _mla_attn_first(qm, kn, kpe128, vt, batch=batch, t=t, tq=512, tk=256)
            ob = _sb_attn(qb, kb16, vb16, batch=batch, t=t, tq=512, tk=256)
        y, nconv = _post(oa, ob, x2, mod3, row(g_out_mla), row(g_out_sb), row(g_norm_ffn), w_out16,
                         w_gate16, w_up16, w_down16, w_conv[l], row(b_conv), cbuf,
                         layer=l, batch=batch, t=t, tm_max=256)
        return y, (lat_st, kpe_st, kb_st, vb_st), nconv

    xp = x_prompt.reshape(bp * tp, d)
    xs = x_sample.reshape(bs * ts, d)
    st_p = st_s = None
    conv_p, conv_s = [], []
    for l in range(depth):
        mod_p = mod[l, :bp].reshape(bp, 1, 6 * d)
        mod_s = mod[l, bp:nb].reshape(bs, 1, 6 * d)
        xp, st_p, nc_p = layer(l, xp, mod_p, cos_p, sin_p, bp, tp, 0, st_p)
        xs, st_s, nc_s = layer(l, xs, mod_s, cos_s, sin_s, bs, ts, past, st_s)
        conv_p.append(nc_p)
        conv_s.append(nc_s)

    def step_outputs(stacks, convs, batch, t):
        lat_st, kpe_st, kb_st, vb_st = stacks
        return (lat_st.reshape(depth, batch, t, D_LATENT), kpe_st.reshape(depth, batch, t, D_ROPE),
                kb_st.reshape(depth, batch, t, H_SB, D_SB), vb_st.reshape(depth, batch, t, H_SB, D_SB),
                jnp.stack(convs))

    return ((xp.reshape(bp, tp, d), xs.reshape(bs, ts, d))
            + step_outputs(st_p, conv_p, bp, tp) + step_outputs(st_s, conv_s, bs, ts))
```

```python
import functools
import math

import jax
import jax.numpy as jnp
from jax import lax
from jax.experimental import pallas as pl
from jax.experimental.pallas import tpu as pltpu

F32 = jnp.float32
BF16 = jnp.bfloat16

H_MLA = 4
D_NOPE = 128
D_ROPE = 64
D_VA = 128
D_LATENT = 512
H_SB = 4
D_SB = 128
CONV_W = 3
CHUNK = 64
ROPE_THETA = 10000.0
EPS = 1e-6

LANES = 128
MXU_COLS = 256
Q_HEAD_COLS = 2 * LANES
MASK_VALUE = -1e30
FF_CHUNKS = 2
ONES_ROWS = 16
VT_ROWS = D_VA + ONES_ROWS
SB_EXP2_ZERO = -152.0
VMEM_LIMIT_BYTES = 56 * 1024 * 1024

MLA_Q_SCALE = math.log2(math.e) * (D_NOPE + D_ROPE) ** -0.5
SB_Q_SCALE = math.log2(math.e) * D_SB ** -0.5


def _rms_rows(x, g):
    return x * lax.rsqrt(jnp.mean(x * x, axis=-1, keepdims=True) + EPS) * g


def _dot(a, b):
    return jnp.dot(a, b, preferred_element_type=F32)


def _dot_nt(a, b):
    return lax.dot_general(a, b, (((1,), (1,)), ((), ())), preferred_element_type=F32)


def _dot_tn(a, b):
    return lax.dot_general(a, b, (((0,), (0,)), ((), ())), preferred_element_type=F32)


def _layer_spec(w, layer):
    return pl.BlockSpec((None,) + w.shape[1:], lambda *_: (layer, 0, 0), pipeline_mode=pl.Buffered(1))


def _params(*sem):
    return pltpu.CompilerParams(dimension_semantics=sem, vmem_limit_bytes=VMEM_LIMIT_BYTES)


def _cast_kernel(x_ref, o_ref):
    o_ref[...] = x_ref[...].astype(o_ref.dtype)


def _to_bf16(w):
    x = w.reshape(-1, w.shape[-1])
    r, c = x.shape
    tm = 512
    while r % tm:
        tm //= 2
    out = pl.pallas_call(
        _cast_kernel,
        grid=(r // tm,),
        in_specs=[pl.BlockSpec((tm, c), lambda i: (i, 0))],
        out_specs=pl.BlockSpec((tm, c), lambda i: (i, 0)),
        out_shape=jax.ShapeDtypeStruct((r, c), BF16),
        compiler_params=_params("parallel"),
        name="cast_bf16",
    )(x)
    return out.reshape(w.shape)


def _ada_kernel(c_ref, w_ref, b_ref, o_ref):
    c = c_ref[...]
    s = (c * jax.nn.sigmoid(c)).astype(BF16)
    o_ref[0] = _dot(s, w_ref[0].astype(BF16)) + b_ref[0]


def _ada_all(c_all, w_ada, b_ada):
    depth, d, n = w_ada.shape
    bp = c_all.shape[0]
    tn = d
    return pl.pallas_call(
        _ada_kernel,
        grid=(depth, n // tn),
        in_specs=[
            pl.BlockSpec((bp, d), lambda l, j: (0, 0)),
            pl.BlockSpec((1, d, tn), lambda l, j: (l, 0, j)),
            pl.BlockSpec((1, 1, tn), lambda l, j: (l, 0, j)),
        ],
        out_specs=pl.BlockSpec((1, bp, tn), lambda l, j: (l, 0, j)),
        out_shape=jax.ShapeDtypeStruct((depth, bp, n), F32),
        compiler_params=_params("parallel", "parallel"),
        name="ada_mod",
    )(c_all, w_ada, b_ada.reshape(depth, 1, n))


C_QM = 0
C_LAT = H_MLA * Q_HEAD_COLS
C_KPE = C_LAT + D_LATENT
C_QB = C_KPE + LANES
C_KB = C_QB + H_SB * D_SB
C_VB = C_KB + H_SB * D_SB
C_END = C_VB + H_SB * D_SB


def _rope_pair(v, g, cos, sin):
    ssq = jnp.sum(v * v, axis=-1, keepdims=True) * (0.5 / D_ROPE)
    vn = v * lax.rsqrt(ssq + EPS) * g
    return vn * cos + pltpu.roll(vn, D_ROPE, 1) * sin


N_INPROJ_IN = 10


def _inproj_kernel(*refs, seg, nseg, d):
    x_ref, mod_ref, gn_ref, w_ref, gqn_ref, gqr_ref, glat_ref, gkr_ref, cos_ref, sin_ref = refs[:N_INPROJ_IN]
    qm_ref, lat_ref, kpe_ref, kpe128_ref, qb_ref, kb_ref, vb_ref, kb16_ref, vb16_ref, h_scr = refs[-10:]
    for s in range(nseg):
        rows = slice(s * seg, (s + 1) * seg)
        m = mod_ref[s]
        h = _rms_rows(x_ref[rows, :], gn_ref[...]) * (1.0 + m[:, d:2 * d]) + m[:, 0:d]
        h_scr[rows, :] = h.astype(BF16)
    h = h_scr[...]
    cos = cos_ref[...]
    sin = sin_ref[...]

    for hd in range(H_MLA):
        c0 = C_QM + hd * Q_HEAD_COLS
        qh = _dot(h, w_ref[:, c0:c0 + Q_HEAD_COLS])
        qn = _rms_rows(qh[:, :D_NOPE], gqn_ref[...]) * MLA_Q_SCALE
        qp = _rope_pair(qh[:, D_NOPE:], gqr_ref[...], cos, sin) * MLA_Q_SCALE
        qm_ref[:, c0:c0 + D_NOPE] = qn.astype(BF16)
        qm_ref[:, c0 + D_NOPE:c0 + Q_HEAD_COLS] = qp.astype(BF16)

    lat_ref[...] = _rms_rows(_dot(h, w_ref[:, C_LAT:C_KPE]), glat_ref[...])

    kp = _rope_pair(_dot(h, w_ref[:, C_KPE:C_QB]), gkr_ref[...], cos, sin)
    kpe_ref[...] = kp[:, :D_ROPE]
    lane = lax.broadcasted_iota(jnp.int32, kp.shape, 1)
    kpe128_ref[...] = jnp.where(lane < D_ROPE, kp, 0.0).astype(BF16)

    qb_ref[...] = (_dot(h, w_ref[:, C_QB:C_KB]) * SB_Q_SCALE).astype(BF16)
    n = h.shape[0]
    kb = _dot(h, w_ref[:, C_KB:C_VB])
    kb16_ref[...] = kb.astype(BF16)
    vb = _dot(h, w_ref[:, C_VB:C_END])
    vb16_ref[...] = vb.astype(BF16)
    for hd in range(H_SB):
        head_rows = pl.ds(hd, n, stride=H_SB)
        kb_ref[head_rows, :] = kb[:, hd * D_SB:(hd + 1) * D_SB]
        vb_ref[head_rows, :] = vb[:, hd * D_SB:(hd + 1) * D_SB]


def _row_tiling(batch, t, tm_max):
    if t >= tm_max:
        assert t % tm_max == 0
        return tm_max, tm_max, 1
    nseg = max(1, min(batch, tm_max // t))
    while batch % nseg:
        nseg -= 1
    return nseg * t, t, nseg


def _mod_spec(t, tm, nseg, width):
    if nseg == 1:
        per = t // tm
        return pl.BlockSpec((1, 1, width), lambda i: (i // per, 0, 0))
    return pl.BlockSpec((nseg, 1, width), lambda i: (i, 0, 0))


STACKED_OUTS = (1, 2, 5, 6)


def _inproj(x2, mod3, gn, w_p, gqn, gqr, glat, gkr, cos, sin, *, batch, t, tm_max, layer, depth, stacks):
    r, d = x2.shape
    tm, seg, nseg = _row_tiling(batch, t, tm_max)
    full = lambda shape: pl.BlockSpec(shape, lambda i: (0,) * len(shape))
    rows = lambda w: pl.BlockSpec((tm, w), lambda i: (i, 0))
    sb = H_SB * D_SB
    outs = ((1, H_MLA * Q_HEAD_COLS, BF16),
            (1, D_LATENT, F32),
            (1, D_ROPE, F32),
            (1, LANES, BF16),
            (1, sb, BF16),
            (H_SB, D_SB, F32), (H_SB, D_SB, F32),
            (1, sb, BF16), (1, sb, BF16))
    out_shapes, out_specs = [], []
    for k, (per, w, dt) in enumerate(outs):
        if k in STACKED_OUTS:
            out_shapes.append(jax.ShapeDtypeStruct((depth, r * per, w), dt))
            out_specs.append(pl.BlockSpec((None, tm * per, w), lambda i: (layer, i, 0)))
        else:
            out_shapes.append(jax.ShapeDtypeStruct((r, w), dt))
            out_specs.append(rows(w))
    in_specs = [rows(d), _mod_spec(t, tm, nseg, 6 * d), full((1, d)), _layer_spec(w_p, layer),
                full((1, D_NOPE)), full((1, LANES)), full((1, D_LATENT)), full((1, LANES)),
                rows(LANES), rows(LANES)]
    operands = (x2, mod3, gn, w_p, gqn, gqr, glat, gkr, cos, sin)
    assert len(operands) == N_INPROJ_IN
    aliases = {}
    if stacks is not None:
        in_specs += [pl.BlockSpec(memory_space=pl.ANY)] * len(STACKED_OUTS)
        operands += tuple(stacks)
        aliases = {N_INPROJ_IN + n: k for n, k in enumerate(STACKED_OUTS)}
    return pl.pallas_call(
        functools.partial(_inproj_kernel, seg=seg, nseg=nseg, d=d),
        grid=(r // tm,),
        in_specs=in_specs,
        out_specs=out_specs,
        out_shape=tuple(out_shapes),
        input_output_aliases=aliases,
        scratch_shapes=[pltpu.VMEM((tm, d), BF16)],
        compiler_params=_params("parallel"),
        name="in_proj",
    )(*operands)


def _kvup_kernel(lat_ref, wuk_ref, wuv_ref, gkn_ref, k_ref, v_ref, *, v_transposed):
    l16 = lat_ref[...].astype(BF16)
    kn = _dot(l16, wuk_ref[...])
    for hd in range(H_MLA):
        cols = slice(hd * D_NOPE, (hd + 1) * D_NOPE)
        k_ref[:, cols] = _rms_rows(kn[:, cols], gkn_ref[...]).astype(BF16)
    if v_transposed:
        vt = _dot_nt(wuv_ref[...], l16).astype(BF16)
        ones = jnp.ones((ONES_ROWS, vt.shape[1]), BF16)
        for hd in range(H_MLA):
            v_ref[hd * VT_ROWS:hd * VT_ROWS + D_VA, :] = vt[hd * D_VA:(hd + 1) * D_VA, :]
            v_ref[hd * VT_ROWS + D_VA:(hd + 1) * VT_ROWS, :] = ones
    else:
        v_ref[...] = _dot(l16, wuv_ref[...]).astype(BF16)


def _kvup(lat_all, wuk, wuv, gkn, *, layer, tm_max, per_batch_t=None, row0=0, n_rows=None):
    r = lat_all.shape[0] if n_rows is None else n_rows
    tm = min(tm_max, r if per_batch_t is None else per_batch_t)
    while r % tm or row0 % tm:
        tm //= 2
    off = row0 // tm
    full = lambda shape: pl.BlockSpec(shape, lambda i: (0,) * len(shape))
    rows = lambda w: pl.BlockSpec((tm, w), lambda i: (i, 0))
    vcols = H_MLA * D_VA
    if per_batch_t is None:
        v_spec, v_shape = rows(vcols), jax.ShapeDtypeStruct((r, vcols), BF16)
    else:
        per = per_batch_t // tm
        v_spec = pl.BlockSpec((None, H_MLA * VT_ROWS, tm), lambda i: (i // per, 0, i % per))
        v_shape = jax.ShapeDtypeStruct((r // per_batch_t, H_MLA * VT_ROWS, per_batch_t), BF16)
    return pl.pallas_call(
        functools.partial(_kvup_kernel, v_transposed=per_batch_t is not None),
        grid=(r // tm,),
        in_specs=[pl.BlockSpec((tm, D_LATENT), lambda i: (i + off, 0)),
                  _layer_spec(wuk, layer), _layer_spec(wuv, layer), full((1, D_NOPE))],
        out_specs=[rows(H_MLA * D_NOPE), v_spec],
        out_shape=(jax.ShapeDtypeStruct((r, H_MLA * D_NOPE), BF16), v_shape),
        compiler_params=_params("parallel"),
        name="kv_up",
    )(lat_all, wuk, wuv, gkn)


def _mla_cached_kernel(q_ref, knp_ref, kpep_ref, vp_ref, knn_ref, kpen_ref, vn_ref, o_ref,
                       m_scr, l_scr, acc_scr, *, t, tk, past):
    m_scr[...] = jnp.full(m_scr.shape, MASK_VALUE, F32)
    l_scr[...] = jnp.zeros(l_scr.shape, F32)
    acc_scr[...] = jnp.zeros(acc_scr.shape, F32)
    qk = D_NOPE + D_ROPE

    def scores(kn, kpe):
        ss = []
        for hd in range(H_MLA):
            kb = jnp.concatenate([kn[:, hd * D_NOPE:(hd + 1) * D_NOPE], kpe], axis=-1)
            ss.append(_dot_nt(kb, q_ref[:, hd * Q_HEAD_COLS:hd * Q_HEAD_COLS + qk]))
        return ss

    def values(ss, v, mask):
        for hd, s in enumerate(ss):
            if mask is not None:
                s = jnp.where(mask, s, MASK_VALUE)
            m_old = m_scr[hd]
            m_new = jnp.maximum(m_old, jnp.max(s, axis=0, keepdims=True))
            alpha = jnp.exp2(m_old - m_new)
            p = jnp.exp2(s - m_new)
            l_scr[hd] = alpha * l_scr[hd] + jnp.sum(p, axis=0, keepdims=True)
            acc_scr[hd] = alpha * acc_scr[hd] + _dot_tn(v[:, hd * D_VA:(hd + 1) * D_VA], p.astype(BF16))
            m_scr[hd] = m_new

    if t > CHUNK:
        kc = lax.broadcasted_iota(jnp.int32, (t, t), 0) // CHUNK
        qc = lax.broadcasted_iota(jnp.int32, (t, t), 1) // CHUNK
        mask = kc <= qc
    else:
        mask = None

    def cached(j):
        return slice(j * tk, (j + 1) * tk)

    nblk = past // tk
    ss = scores(knp_ref[cached(0), :], kpep_ref[cached(0), :].astype(BF16)) if nblk else None
    for j in range(nblk):
        if j + 1 < nblk:
            ss_next = scores(knp_ref[cached(j + 1), :], kpep_ref[cached(j + 1), :].astype(BF16))
        else:
            ss_next = scores(knn_ref[...], kpen_ref[:, 0:D_ROPE])
        values(ss, vp_ref[cached(j), :], None)
        ss = ss_next
    if not nblk:
        ss = scores(knn_ref[...], kpen_ref[:, 0:D_ROPE])
    values(ss, vn_ref[...], mask)

    for hd in range(H_MLA):
        o = acc_scr[hd] * (1.0 / l_scr[hd])
        o_ref[:, hd * D_VA:(hd + 1) * D_VA] = o.T


def _mla_attn_cached(q, knp, kpe_cache, vp, knn, kpen, vn, *, layer, batch, t, past, tk):
    assert past % tk == 0 and past % CHUNK == 0 and t % CHUNK == 0
    vcols = H_MLA * D_VA
    new = lambda a: pl.BlockSpec((t, a.shape[1]), lambda b: (b, 0))
    old = lambda a: pl.BlockSpec((past, a.shape[1]), lambda b: (b, 0))
    scratch = [pltpu.VMEM((H_MLA, 1, t), F32), pltpu.VMEM((H_MLA, 1, t), F32), pltpu.VMEM((H_MLA, D_VA, t), F32)]
    return pl.pallas_call(
        functools.partial(_mla_cached_kernel, t=t, tk=tk, past=past),
        grid=(batch,),
        in_specs=[new(q), old(knp),
                  pl.BlockSpec((None, None, past, D_ROPE), lambda b: (layer, b, 0, 0)),
                  old(vp), new(knn), new(kpen), new(vn)],
        out_specs=pl.BlockSpec((t, vcols), lambda b: (b, 0)),
        out_shape=jax.ShapeDtypeStruct((batch * t, vcols), F32),
        scratch_shapes=scratch,
        compiler_params=_params("parallel"),
        name="mla_attn_cached",
    )(q, knp, kpe_cache, vp, knn, kpen, vn)


def _mla_first_kernel(q_ref, kn_ref, kpe_ref, vt_ref, o_ref, qt_scr, m_scr, acc_scr, s_scr, *, tq, tk):
    i = pl.program_id(1)
    nb = tq // tk
    assert nb == 2
    for hd in range(H_MLA):
        qt_scr[hd] = q_ref[:, hd * Q_HEAD_COLS:(hd + 1) * Q_HEAD_COLS].T
    m_scr[...] = jnp.full(m_scr.shape, MASK_VALUE, F32)
    acc_scr[...] = jnp.zeros(acc_scr.shape, F32)
    last = i * nb + nb - 1

    def scores(b, buf):
        r0 = pl.multiple_of(b * tk, tk)
        kpe = kpe_ref[pl.ds(r0, tk), :]
        for hd in range(H_MLA):
            kb = jnp.concatenate([kn_ref[pl.ds(r0, tk), hd * D_NOPE:(hd + 1) * D_NOPE], kpe], axis=-1)
            s = _dot(kb, qt_scr[hd])
            for g in range(tq // LANES):
                s_scr[buf, hd, g] = s[:, g * LANES:(g + 1) * LANES]

    def values(b, buf, own):
        r0 = pl.multiple_of(b * tk, tk)
        for hd in range(H_MLA):
            vt = vt_ref[hd * VT_ROWS:(hd + 1) * VT_ROWS, pl.ds(r0, tk)]
            for w0 in range(0, tq, MXU_COLS):
                ps, alphas = [], []
                for c0 in range(w0, w0 + MXU_COLS, LANES):
                    qs = slice(c0, c0 + LANES)
                    s = s_scr[buf, hd, c0 // LANES]
                    if own is not None:
                        kc = lax.broadcasted_iota(jnp.int32, (tk, LANES), 0) // CHUNK + own * (tk // CHUNK)
                        qc = (lax.broadcasted_iota(jnp.int32, (tk, LANES), 1) + c0) // CHUNK
                        s = jnp.where(kc <= qc, s, MASK_VALUE)
                    m_old = m_scr[hd, :, qs]
                    m_new = jnp.maximum(m_old, jnp.max(s, axis=0, keepdims=True))
                    m_scr[hd, :, qs] = m_new
                    alphas.append(jnp.exp2(m_old - m_new))
                    ps.append(jnp.exp2(s - m_new).astype(BF16))
                w = w0 // MXU_COLS
                acc_scr[hd, w] = (jnp.concatenate(alphas, axis=1) * acc_scr[hd, w]
                                  + _dot(vt, jnp.concatenate(ps, axis=1)))

    def step(t, buf, own=None):
        scores(t, buf)
        values(t - 1, 1 - buf, own)

    scores(0, 0)

    def pair(k, carry):
        t = 1 + 2 * k
        step(t, 1)
        step(t + 1, 0)
        return carry

    def four_pairs(k, carry):
        for n in range(4):
            carry = pair(4 * k + n, carry)
        return carry

    lax.fori_loop(0, i // 4, four_pairs, 0)
    lax.fori_loop(i // 4 * 4, i, pair, 0)
    step(last, 1, own=0)
    values(last, 1, 1)

    for hd in range(H_MLA):
        for w in range(tq // MXU_COLS):
            o = acc_scr[hd, w, 0:D_VA, :] * (1.0 / acc_scr[hd, w, D_VA:D_VA + 1, :])
            o_ref[w * MXU_COLS:(w + 1) * MXU_COLS, hd * D_VA:(hd + 1) * D_VA] = o.T


def _mla_attn_first(q, kn, kpe, vt, *, batch, t, tq, tk):
    assert t % tq == 0 and tq == 2 * tk and tk % CHUNK == 0
    nq = t // tq
    vcols = H_MLA * D_VA
    resident = lambda a: pl.BlockSpec((t, a.shape[1]), lambda b, i: (b, 0), pipeline_mode=pl.Buffered(1))
    scratch = [pltpu.VMEM((H_MLA, Q_HEAD_COLS, tq), BF16),
               pltpu.VMEM((H_MLA, 1, tq), F32),
               pltpu.VMEM((H_MLA, tq // MXU_COLS, VT_ROWS, MXU_COLS), F32),
               pltpu.VMEM((2, H_MLA, tq // LANES, tk, LANES), F32)]
    return pl.pallas_call(
        functools.partial(_mla_first_kernel, tq=tq, tk=tk),
        grid=(batch, nq),
        in_specs=[pl.BlockSpec((tq, q.shape[1]), lambda b, i: (b * nq + i, 0)), resident(kn), resident(kpe),
                  pl.BlockSpec((None, H_MLA * VT_ROWS, t), lambda b, i: (b, 0, 0), pipeline_mode=pl.Buffered(1))],
        out_specs=pl.BlockSpec((tq, vcols), lambda b, i: (b * nq + i, 0)),
        out_shape=jax.ShapeDtypeStruct((batch * t, vcols), F32),
        scratch_shapes=scratch,
        compiler_params=_params("parallel", "arbitrary"),
        name="mla_attn_first",
    )(q, kn, kpe, vt)


def _sb_kernel(q_ref, *refs, tq, tsub, tk, past):
    if past:
        kp_ref, vp_ref, k_ref, v_ref, o_ref, c_scr, acc_scr = refs
    else:
        k_ref, v_ref, o_ref, c_scr, acc_scr = refs
    i = pl.program_id(1)
    nsub = tq // tsub
    c_scr[...] = jnp.zeros(c_scr.shape, F32)
    acc_scr[...] = jnp.zeros(acc_scr.shape, F32)
    heads = [slice(hd * D_SB, (hd + 1) * D_SB) for hd in range(H_SB)]
    chains = [(u, hd) for u in range(nsub) for hd in range(H_SB)]

    def later_eq(n):
        return (lax.broadcasted_iota(jnp.int32, (n, n), 1)
                >= lax.broadcasted_iota(jnp.int32, (n, n), 0)).astype(BF16)

    def new_rows(r0, n):
        return ([k_ref[pl.ds(r0, n), cols] for cols in heads], [v_ref[pl.ds(r0, n), cols] for cols in heads])

    def cached_rows(r0, n):
        return ([kp_ref[pl.ds(r0, n), hd, :].astype(BF16) for hd in range(H_SB)],
                [vp_ref[pl.ds(r0, n), hd, :].astype(BF16) for hd in range(H_SB)])

    def update(kvs, live, tri, before):
        zs = [_dot_nt(kvs[u][0][hd], q_ref[u * tsub:(u + 1) * tsub, heads[hd]]) for u, hd in chains]
        es, incls = [], []
        for z in zs:
            ls = jnp.minimum(-z, 0.0) - jnp.log2(1.0 + jnp.exp2(-jnp.abs(z)))
            lsm = ls if before is None else jnp.where(before, ls, 0.0)
            incl = _dot(tri, lsm.astype(BF16))
            incls.append(incl)
            es.append(z + incl if before is None else z + ls + (incl - lsm))
        top = None
        for n, (u, hd) in enumerate(chains):
            c = c_scr[u, hd]
            e = es[n] + c
            if live[u] is not None:
                e = jnp.where(live[u], e, MASK_VALUE)
            w = jnp.exp2(e)
            if before is not None:
                w = jnp.where(before, w, 0.0)
            acc_scr[u, hd] += _dot_tn(kvs[u][1][hd], w.astype(BF16))
            c_new = c + incls[n][0:1, :]
            c_scr[u, hd] = c_new
            top_c = jnp.max(c_new)
            if live[u] is not None:
                top_c = jnp.where(live[u], top_c, MASK_VALUE)
            top = top_c if top is None else jnp.maximum(top, top_c)
        return top

    own0 = i * tq
    before = (lax.broadcasted_iota(jnp.int32, (tsub, tsub), 0)
              < lax.broadcasted_iota(jnp.int32, (tsub, tsub), 1))
    top = update([new_rows(pl.multiple_of(own0 + u * tsub, tsub), tsub) for u in range(nsub)],
                 [None] * nsub, later_eq(tsub), before)
    tri_k = later_eq(tk)
    older = cached_rows if past else new_rows
    newest = [(past + own0 + u * tsub) // tk - 1 for u in range(nsub)]

    def cond(st):
        k, top = st
        return jnp.logical_and(newest[-1] - k >= 0, top > SB_EXP2_ZERO)

    def body(st):
        k, _ = st
        kvs, live = [], []
        for u in range(nsub):
            j = newest[u] - k
            kvs.append(older(pl.multiple_of(jnp.maximum(j, 0) * tk, tk), tk))
            live.append(None if u == nsub - 1 else j >= 0)
        return k + 1, update(kvs, live, tri_k, None)

    lax.while_loop(cond, body, (0, top))
    for u, hd in chains:
        o_ref[u * tsub:(u + 1) * tsub, heads[hd]] = acc_scr[u, hd].T


def _sb_attn(q, k, v, *, batch, t, tq, tk, past=0, k_cache=None, v_cache=None, layer=0):
    tsub = tq if past else tk
    assert t % tq == 0 and past % tk == 0 and tq % tsub == 0 and (past == 0 or t == tq)
    nq = t // tq
    nsub = tq // tsub
    cols = H_SB * D_SB
    q_spec = pl.BlockSpec((tq, cols), lambda b, i: (b * nq + i, 0))
    if past:
        cache = pl.BlockSpec((None, None, past, H_SB, D_SB), lambda b, i: (layer, b, 0, 0, 0))
        new = pl.BlockSpec((t, cols), lambda b, i: (b, 0))
        in_specs, operands = [q_spec, cache, cache, new, new], (q, k_cache, v_cache, k, v)
    else:
        resident = pl.BlockSpec((t, cols), lambda b, i: (b, 0), pipeline_mode=pl.Buffered(1))
        in_specs, operands = [q_spec, resident, resident], (q, k, v)
    return pl.pallas_call(
        functools.partial(_sb_kernel, tq=tq, tsub=tsub, tk=tk, past=past),
        grid=(batch, nq),
        in_specs=in_specs,
        out_specs=q_spec,
        out_shape=jax.ShapeDtypeStruct((batch * t, cols), F32),
        scratch_shapes=[pltpu.VMEM((nsub, H_SB, 1, tsub), F32), pltpu.VMEM((nsub, H_SB, D_SB, tsub), F32)],
        compiler_params=_params("parallel", "arbitrary"),
        name="sb_attn",
    )(*operands)


def _post_kernel(oa_ref, ob_ref, x_ref, mod_ref, goa_ref, gob_ref, gnf_ref, wout_ref, wg_ref, wu_ref, wd_ref,
                 wconv_ref, bconv_ref, cbuf_ref, y_ref, nconv_ref, x1_scr, h_scr, u_scr,
                 *, seg, nseg, d, per_batch):
    i = pl.program_id(0)
    na = oa_ref.shape[1]
    ma = _rms_rows(oa_ref[...], goa_ref[...]).astype(BF16)
    mb = _rms_rows(ob_ref[...], gob_ref[...]).astype(BF16)
    mix = _dot(ma, wout_ref[0:na, :]) + _dot(mb, wout_ref[na:, :])
    for s in range(nseg):
        rows = slice(s * seg, (s + 1) * seg)
        m = mod_ref[s]
        x1 = x_ref[rows, :] + m[:, 2 * d:3 * d] * mix[rows, :]
        x1_scr[rows, :] = x1
        h = _rms_rows(x1, gnf_ref[...]) * (1.0 + m[:, 4 * d:5 * d]) + m[:, 3 * d:4 * d]
        h_scr[rows, :] = h.astype(BF16)
    h = h_scr[...]

    if nseg == 1:
        first = i % per_batch == 0

        @pl.when(first)
        def _():
            u_scr[0, 8 - (CONV_W - 1):8, :] = cbuf_ref[0]

        @pl.when(jnp.logical_not(first))
        def _():
            u_scr[0, 0:8, :] = u_scr[0, seg:seg + 8, :]
    else:
        for s in range(nseg):
            u_scr[s, 8 - (CONV_W - 1):8, :] = cbuf_ref[s]

    ff = wg_ref.shape[1]
    bounds = [ff * c // FF_CHUNKS // MXU_COLS * MXU_COLS for c in range(FF_CHUNKS)] + [ff]
    chunks = [slice(bounds[c], bounds[c + 1]) for c in range(FF_CHUNKS)]
    us = [_dot(h, wg_ref[:, c]) for c in chunks]
    ups = [_dot(h, wu_ref[:, c]) for c in chunks]
    down = None
    for c, u, up in zip(chunks, us, ups):
        for s in range(nseg):
            u_scr[s, 8:8 + seg, c] = u[s * seg:(s + 1) * seg, :]
            nconv_ref[s, :, c] = u[(s + 1) * seg - (CONV_W - 1):(s + 1) * seg, :]
        parts = []
        for s in range(nseg):
            uc = bconv_ref[:, c]
            for tap in range(CONV_W):
                lo = 8 - (CONV_W - 1) + tap
                uc = uc + u_scr[s, lo:lo + seg, c] * wconv_ref[tap:tap + 1, c]
            parts.append(uc)
        uc = parts[0] if nseg == 1 else jnp.concatenate(parts, axis=0)
        act = (uc * jax.nn.sigmoid(uc) * up).astype(BF16)
        part = _dot(act, wd_ref[c, :])
        down = part if down is None else down + part
    for s in range(nseg):
        rows = slice(s * seg, (s + 1) * seg)
        y_ref[rows, :] = x1_scr[rows, :] + mod_ref[s][:, 5 * d:6 * d] * down[rows, :]


def _post(oa, ob, x2, mod3, goa, gob, gnf, wout, wg, wu, wd, wconv, bconv, cbuf, *, layer, batch, t, tm_max):
    r, d = x2.shape
    ff = wg.shape[-1]
    tm, seg, nseg = _row_tiling(batch, t, tm_max)
    per_batch = max(1, t // tm)
    full = lambda shape: pl.BlockSpec(shape, lambda i: (0,) * len(shape))
    rows = lambda w: pl.BlockSpec((tm, w), lambda i: (i, 0))
    if nseg == 1:
        state_spec = pl.BlockSpec((1, CONV_W - 1, ff), lambda i: (i // per_batch, 0, 0))
    else:
        state_spec = pl.BlockSpec((nseg, CONV_W - 1, ff), lambda i: (i, 0, 0))
    return pl.pallas_call(
        functools.partial(_post_kernel, seg=seg, nseg=nseg, d=d, per_batch=per_batch),
        grid=(r // tm,),
        in_specs=[rows(oa.shape[1]), rows(ob.shape[1]), rows(d), _mod_spec(t, tm, nseg, 6 * d),
                  full((1, oa.shape[1])), full((1, ob.shape[1])), full((1, d)),
                  _layer_spec(wout, layer), _layer_spec(wg, layer), _layer_spec(wu, layer),
                  _layer_spec(wd, layer),
                  full((CONV_W, ff)), full((1, ff)), state_spec],
        out_specs=[rows(d), state_spec],
        out_shape=(jax.ShapeDtypeStruct((r, d), F32),
                   jax.ShapeDtypeStruct((batch, CONV_W - 1, ff), F32)),
        scratch_shapes=[pltpu.VMEM((tm, d), F32), pltpu.VMEM((tm, d), BF16),
                        pltpu.VMEM((nseg, seg + 8, ff), F32)],
        compiler_params=_params("arbitrary"),
        name="post_ffn",
    )(oa, ob, x2, mod3, goa, gob, gnf, wout, wg, wu, wd, wconv, bconv, cbuf)


def _in_col_runs():
    qa = H_MLA * (D_NOPE + D_ROPE)
    half = D_ROPE // 2
    runs = []
    for h in range(H_MLA):
        base = h * (D_NOPE + D_ROPE)
        pe = base + D_NOPE
        runs += [(base, pe + D_ROPE), (pe + half, pe + D_ROPE), (pe, pe + half)]
    runs.append((qa, qa + D_LATENT))
    kp = qa + D_LATENT
    runs += [(kp, kp + D_ROPE), (kp + half, kp + D_ROPE), (kp, kp + half)]
    runs.append((kp + D_ROPE, kp + D_ROPE + 3 * H_SB * D_SB))
    return runs


def _win_kernel(w_ref, o_ref):
    w = w_ref[...]
    o_ref[...] = jnp.concatenate([w[:, a:b] for a, b in _in_col_runs()], axis=1).astype(o_ref.dtype)


def _permute_w_in(w_in):
    depth, d, n = w_in.shape
    x = w_in.reshape(depth * d, n)
    tm = 256
    out = pl.pallas_call(
        _win_kernel,
        grid=(depth * d // tm,),
        in_specs=[pl.BlockSpec((tm, n), lambda i: (i, 0))],
        out_specs=pl.BlockSpec((tm, C_END), lambda i: (i, 0)),
        out_shape=jax.ShapeDtypeStruct((depth * d, C_END), BF16),
        compiler_params=_params("parallel"),
        name="permute_w_in",
    )(x)
    return out.reshape(depth, d, C_END)


def _rope_gain(g):
    half = D_ROPE // 2
    return jnp.concatenate([g, g[..., half:], g[..., :half]], axis=-1)


def _rope_tables(pos0, t):
    half = D_ROPE // 2
    freqs = ROPE_THETA ** (-jnp.arange(half, dtype=F32) / half)

    def direct(pos):
        ang = pos.astype(F32)[:, None] * freqs[None, :]
        return jnp.cos(ang), jnp.sin(ang)

    if t % LANES == 0 and pos0 % LANES == 0 and t > LANES:
        ca, sa = direct(pos0 + LANES * jnp.arange(t // LANES, dtype=jnp.int32))
        cb, sb = direct(jnp.arange(LANES, dtype=jnp.int32))
        cos = (ca[:, None] * cb[None] - sa[:, None] * sb[None]).reshape(t, half)
        sin = (sa[:, None] * cb[None] + ca[:, None] * sb[None]).reshape(t, half)
    else:
        cos, sin = direct(pos0 + jnp.arange(t, dtype=jnp.int32))
    return jnp.tile(cos, (1, 4)), jnp.concatenate([-sin, sin, sin, -sin], axis=-1)


def kernel(x_prompt, x_sample, c_prompt, c_sample, cache_mla_latent, cache_mla_krope, cache_sb_k, cache_sb_v, state_ffn_conv, w_ada, b_ada, g_norm_mix, g_norm_ffn, w_in, g_kv_latent, g_q_nope, g_q_rope, g_k_nope, g_k_rope, w_uk, w_uv, g_out_mla, g_out_sb, w_out, w_gate, w_up, w_conv, b_conv, w_down):
    depth = w_in.shape[0]
    bp, tp, d = x_prompt.shape
    bs, ts, _ = x_sample.shape
    past = cache_mla_latent.shape[2]
    ff = w_gate.shape[2]
    sb = H_SB * D_SB

    nb = bp + bs
    nb_pad = -(-nb // 8) * 8
    c_all = jnp.concatenate([c_prompt, c_sample, jnp.zeros((nb_pad - nb, d), F32)], axis=0)
    mod = _ada_all(c_all, w_ada, b_ada)

    w_in_p = _permute_w_in(w_in)
    w_uk16, w_uv16, w_out16 = _to_bf16(w_uk), _to_bf16(w_uv), _to_bf16(w_out)
    w_uvt16 = jnp.swapaxes(w_uv16, 1, 2)
    w_gate16, w_up16, w_down16 = _to_bf16(w_gate), _to_bf16(w_up), _to_bf16(w_down)
    gqr, gkr = _rope_gain(g_q_rope), _rope_gain(g_k_rope)

    cos_p, sin_p = _rope_tables(0, tp)
    cos_p, sin_p = jnp.tile(cos_p, (bp, 1)), jnp.tile(sin_p, (bp, 1))
    cos_s, sin_s = _rope_tables(past, ts)
    cos_s, sin_s = jnp.tile(cos_s, (bs, 1)), jnp.tile(sin_s, (bs, 1))

    lat_cache_rows = cache_mla_latent.reshape(-1, D_LATENT)

    def layer(l, x2, mod3, cos, sin, batch, t, p, stacks):
        row = lambda a: a[l][None, :]
        qm, lat_st, kpe_st, kpe128, qb, kb_st, vb_st, kb16, vb16 = _inproj(
            x2, mod3, row(g_norm_mix), w_in_p, row(g_q_nope), row(gqr), row(g_kv_latent), row(gkr),
            cos, sin, batch=batch, t=t, tm_max=512, layer=l, depth=depth, stacks=stacks)
        new_lat = dict(row0=l * batch * t, n_rows=batch * t)
        lat_rows = lat_st.reshape(-1, D_LATENT)
        if p:
            cbuf = state_ffn_conv[l]
            knp, vp = _kvup(lat_cache_rows, w_uk16, w_uv16, row(g_k_nope), layer=l, tm_max=2048,
                            row0=l * batch * p, n_rows=batch * p)
            knn, vn = _kvup(lat_rows, w_uk16, w_uv16, row(g_k_nope), layer=l, tm_max=512, **new_lat)
            oa = _mla_attn_cached(qm, knp, cache_mla_krope, vp, knn, kpe128, vn,
                                  layer=l, batch=batch, t=t, past=p, tk=256)
            ob = _sb_attn(qb, kb16, vb16, batch=batch, t=t, tq=t, tk=256, past=p,
                          k_cache=cache_sb_k, v_cache=cache_sb_v, layer=l)
        else:
            cbuf = jnp.zeros((batch, CONV_W - 1, ff), F32)
            kn, vt = _kvup(lat_rows, w_uk16, w_uvt16, row(g_k_nope), layer=l, tm_max=2048, per_batch_t=t,
                           **new_lat)
            oa = _mla_attn_first(qm, kn, kpe128, vt, batch=batch, t=t, tq=512, tk=256)
            ob = _sb_attn(qb, kb16, vb16, batch=batch, t=t, tq=512, tk=256)
        y, nconv = _post(oa, ob, x2, mod3, row(g_out_mla), row(g_out_sb), row(g_norm_ffn), w_out16,
                         w_gate16, w_up16, w_down16, w_conv[l], row(b_conv), cbuf,
                         layer=l, batch=batch, t=t, tm_max=256)
        return y, (lat_st, kpe_st, kb_st, vb_st), nconv

    xp = x_prompt.reshape(bp * tp, d)
    xs = x_sample.reshape(bs * ts, d)
    st_p = st_s = None
    conv_p, conv_s = [], []
    for l in range(depth):
        mod_p = mod[l, :bp].reshape(bp, 1, 6 * d)
        mod_s = mod[l, bp:nb].reshape(bs, 1, 6 * d)
        xp, st_p, nc_p = layer(l, xp, mod_p, cos_p, sin_p, bp, tp, 0, st_p)
        xs, st_s, nc_s = layer(l, xs, mod_s, cos_s, sin_s, bs, ts, past, st_s)
        conv_p.append(nc_p)
        conv_s.append(nc_s)

    def step_outputs(stacks, convs, batch, t):
        lat_st, kpe_st, kb_st, vb_st = stacks
        return (lat_st.reshape(depth, batch, t, D_LATENT), kpe_st.reshape(depth, batch, t, D_ROPE),
                kb_st.reshape(depth, batch, t, H_SB, D_SB), vb_st.reshape(depth, batch, t, H_SB, D_SB),
                jnp.stack(convs))

    return ((xp.reshape(bp, tp, d), xs.reshape(bs, ts, d))
            + step_outputs(st_p, conv_p, bp, tp) + step_outputs(st_s, conv_s, bs, ts))
```

```python
import functools
import math

import jax
import jax.numpy as jnp
from jax import lax
from jax.experimental import pallas as pl
from jax.experimental.pallas import tpu as pltpu

F32 = jnp.float32
BF16 = jnp.bfloat16

H_MLA = 4
D_NOPE = 128
D_ROPE = 64
D_VA = 128
D_LATENT = 512
H_SB = 4
D_SB = 128
CONV_W = 3
CHUNK = 64
ROPE_THETA = 10000.0
EPS = 1e-6

LANES = 128
MXU_COLS = 256
Q_HEAD_COLS = 2 * LANES
MASK_VALUE = -1e30
FF_CHUNKS = 2
ONES_ROWS = 16
VT_ROWS = D_VA + ONES_ROWS
SB_EXP2_ZERO = -152.0
VMEM_LIMIT_BYTES = 56 * 1024 * 1024

MLA_Q_SCALE = math.log2(math.e) * (D_NOPE + D_ROPE) ** -0.5
SB_Q_SCALE = math.log2(math.e) * D_SB ** -0.5


def _rms_rows(x, g):
    return x * lax.rsqrt(jnp.mean(x * x, axis=-1, keepdims=True) + EPS) * g


def _dot(a, b):
    return jnp.dot(a, b, preferred_element_type=F32)


def _dot_nt(a, b):
    return lax.dot_general(a, b, (((1,), (1,)), ((), ())), preferred_element_type=F32)


def _dot_tn(a, b):
    return lax.dot_general(a, b, (((0,), (0,)), ((), ())), preferred_element_type=F32)


def _layer_spec(w, layer):
    return pl.BlockSpec((None,) + w.shape[1:], lambda *_: (layer, 0, 0), pipeline_mode=pl.Buffered(1))


def _params(*sem):
    return pltpu.CompilerParams(dimension_semantics=sem, vmem_limit_bytes=VMEM_LIMIT_BYTES)


def _cast_kernel(x_ref, o_ref):
    o_ref[...] = x_ref[...].astype(o_ref.dtype)


def _to_bf16(w):
    x = w.reshape(-1, w.shape[-1])
    r, c = x.shape
    tm = 512
    while r % tm:
        tm //= 2
    out = pl.pallas_call(
        _cast_kernel,
        grid=(r // tm,),
        in_specs=[pl.BlockSpec((tm, c), lambda i: (i, 0))],
        out_specs=pl.BlockSpec((tm, c), lambda i: (i, 0)),
        out_shape=jax.ShapeDtypeStruct((r, c), BF16),
        compiler_params=_params("parallel"),
        name="cast_bf16",
    )(x)
    return out.reshape(w.shape)


def _ada_kernel(c_ref, w_ref, b_ref, o_ref):
    c = c_ref[...]
    s = (c * jax.nn.sigmoid(c)).astype(BF16)
    o_ref[0] = _dot(s, w_ref[0].astype(BF16)) + b_ref[0]


def _ada_all(c_all, w_ada, b_ada):
    depth, d, n = w_ada.shape
    bp = c_all.shape[0]
    tn = d
    return pl.pallas_call(
        _ada_kernel,
        grid=(depth, n // tn),
        in_specs=[
            pl.BlockSpec((bp, d), lambda l, j: (0, 0)),
            pl.BlockSpec((1, d, tn), lambda l, j: (l, 0, j)),
            pl.BlockSpec((1, 1, tn), lambda l, j: (l, 0, j)),
        ],
        out_specs=pl.BlockSpec((1, bp, tn), lambda l, j: (l, 0, j)),
        out_shape=jax.ShapeDtypeStruct((depth, bp, n), F32),
        compiler_params=_params("parallel", "parallel"),
        name="ada_mod",
    )(c_all, w_ada, b_ada.reshape(depth, 1, n))


C_QM = 0
C_LAT = H_MLA * Q_HEAD_COLS
C_KPE = C_LAT + D_LATENT
C_QB = C_KPE + LANES
C_KB = C_QB + H_SB * D_SB
C_VB = C_KB + H_SB * D_SB
C_END = C_VB + H_SB * D_SB


def _rope_pair(v, g, cos, sin):
    ssq = jnp.sum(v * v, axis=-1, keepdims=True) * (0.5 / D_ROPE)
    vn = v * lax.rsqrt(ssq + EPS) * g
    return vn * cos + pltpu.roll(vn, D_ROPE, 1) * sin


N_INPROJ_IN = 10


def _inproj_kernel(*refs, seg, nseg, d):
    x_ref, mod_ref, gn_ref, w_ref, gqn_ref, gqr_ref, glat_ref, gkr_ref, cos_ref, sin_ref = refs[:N_INPROJ_IN]
    qm_ref, lat_ref, kpe_ref, kpe128_ref, qb_ref, kb_ref, vb_ref, kb16_ref, vb16_ref, h_scr = refs[-10:]
    for s in range(nseg):
        rows = slice(s * seg, (s + 1) * seg)
        m = mod_ref[s]
        h = _rms_rows(x_ref[rows, :], gn_ref[...]) * (1.0 + m[:, d:2 * d]) + m[:, 0:d]
        h_scr[rows, :] = h.astype(BF16)
    h = h_scr[...]
    cos = cos_ref[...]
    sin = sin_ref[...]

    for hd in range(H_MLA):
        c0 = C_QM + hd * Q_HEAD_COLS
        qh = _dot(h, w_ref[:, c0:c0 + Q_HEAD_COLS])
        qn = _rms_rows(qh[:, :D_NOPE], gqn_ref[...]) * MLA_Q_SCALE
        qp = _rope_pair(qh[:, D_NOPE:], gqr_ref[...], cos, sin) * MLA_Q_SCALE
        qm_ref[:, c0:c0 + D_NOPE] = qn.astype(BF16)
        qm_ref[:, c0 + D_NOPE:c0 + Q_HEAD_COLS] = qp.astype(BF16)

    lat_ref[...] = _rms_rows(_dot(h, w_ref[:, C_LAT:C_KPE]), glat_ref[...])

    kp = _rope_pair(_dot(h, w_ref[:, C_KPE:C_QB]), gkr_ref[...], cos, sin)
    kpe_ref[...] = kp[:, :D_ROPE]
    lane = lax.broadcasted_iota(jnp.int32, kp.shape, 1)
    kpe128_ref[...] = jnp.where(lane < D_ROPE, kp, 0.0).astype(BF16)

    qb_ref[...] = (_dot(h, w_ref[:, C_QB:C_KB]) * SB_Q_SCALE).astype(BF16)
    n = h.shape[0]
    kb = _dot(h, w_ref[:, C_KB:C_VB])
    kb16_ref[...] = kb.astype(BF16)
    vb = _dot(h, w_ref[:, C_VB:C_END])
    vb16_ref[...] = vb.astype(BF16)
    for hd in range(H_SB):
        head_rows = pl.ds(hd, n, stride=H_SB)
        kb_ref[head_rows, :] = kb[:, hd * D_SB:(hd + 1) * D_SB]
        vb_ref[head_rows, :] = vb[:, hd * D_SB:(hd + 1) * D_SB]


def _row_tiling(batch, t, tm_max):
    if t >= tm_max:
        assert t % tm_max == 0
        return tm_max, tm_max, 1
    nseg = max(1, min(batch, tm_max // t))
    while batch % nseg:
        nseg -= 1
    return nseg * t, t, nseg


def _mod_spec(t, tm, nseg, width):
    if nseg == 1:
        per = t // tm
        return pl.BlockSpec((1, 1, width), lambda i: (i // per, 0, 0))
    return pl.BlockSpec((nseg, 1, width), lambda i: (i, 0, 0))


STACKED_OUTS = (1, 2, 5, 6)


def _inproj(x2, mod3, gn, w_p, gqn, gqr, glat, gkr, cos, sin, *, batch, t, tm_max, layer, depth, stacks):
    r, d = x2.shape
    tm, seg, nseg = _row_tiling(batch, t, tm_max)
    full = lambda shape: pl.BlockSpec(shape, lambda i: (0,) * len(shape))
    rows = lambda w: pl.BlockSpec((tm, w), lambda i: (i, 0))
    sb = H_SB * D_SB
    outs = ((1, H_MLA * Q_HEAD_COLS, BF16),
            (1, D_LATENT, F32),
            (1, D_ROPE, F32),
            (1, LANES, BF16),
            (1, sb, BF16),
            (H_SB, D_SB, F32), (H_SB, D_SB, F32),
            (1, sb, BF16), (1, sb, BF16))
    out_shapes, out_specs = [], []
    for k, (per, w, dt) in enumerate(outs):
        if k in STACKED_OUTS:
            out_shapes.append(jax.ShapeDtypeStruct((depth, r * per, w), dt))
            out_specs.append(pl.BlockSpec((None, tm * per, w), lambda i: (layer, i, 0)))
        else:
            out_shapes.append(jax.ShapeDtypeStruct((r, w), dt))
            out_specs.append(rows(w))
    in_specs = [rows(d), _mod_spec(t, tm, nseg, 6 * d), full((1, d)), _layer_spec(w_p, layer),
                full((1, D_NOPE)), full((1, LANES)), full((1, D_LATENT)), full((1, LANES)),
                rows(LANES), rows(LANES)]
    operands = (x2, mod3, gn, w_p, gqn, gqr, glat, gkr, cos, sin)
    assert len(operands) == N_INPROJ_IN
    aliases = {}
    if stacks is not None:
        in_specs += [pl.BlockSpec(memory_space=pl.ANY)] * len(STACKED_OUTS)
        operands += tuple(stacks)
        aliases = {N_INPROJ_IN + n: k for n, k in enumerate(STACKED_OUTS)}
    return pl.pallas_call(
        functools.partial(_inproj_kernel, seg=seg, nseg=nseg, d=d),
        grid=(r // tm,),
        in_specs=in_specs,
        out_specs=out_specs,
        out_shape=tuple(out_shapes),
        input_output_aliases=aliases,
        scratch_shapes=[pltpu.VMEM((tm, d), BF16)],
        compiler_params=_params("parallel"),
        name="in_proj",
    )(*operands)


def _kvup_kernel(lat_ref, wuk_ref, wuv_ref, gkn_ref, k_ref, v_ref, *, v_transposed):
    l16 = lat_ref[...].astype(BF16)
    kn = _dot(l16, wuk_ref[...])
    for hd in range(H_MLA):
        cols = slice(hd * D_NOPE, (hd + 1) * D_NOPE)
        k_ref[:, cols] = _rms_rows(kn[:, cols], gkn_ref[...]).astype(BF16)
    if v_transposed:
        vt = _dot_nt(wuv_ref[...], l16).astype(BF16)
        ones = jnp.ones((ONES_ROWS, vt.shape[1]), BF16)
        for hd in range(H_MLA):
            v_ref[hd * VT_ROWS:hd * VT_ROWS + D_VA, :] = vt[hd * D_VA:(hd + 1) * D_VA, :]
            v_ref[hd * VT_ROWS + D_VA:(hd + 1) * VT_ROWS, :] = ones
    else:
        v_ref[...] = _dot(l16, wuv_ref[...]).astype(BF16)


def _kvup(lat_all, wuk, wuv, gkn, *, layer, tm_max, per_batch_t=None, row0=0, n_rows=None):
    r = lat_all.shape[0] if n_rows is None else n_rows
    tm = min(tm_max, r if per_batch_t is None else per_batch_t)
    while r % tm or row0 % tm:
        tm //= 2
    off = row0 // tm
    full = lambda shape: pl.BlockSpec(shape, lambda i: (0,) * len(shape))
    rows = lambda w: pl.BlockSpec((tm, w), lambda i: (i, 0))
    vcols = H_MLA * D_VA
    if per_batch_t is None:
        v_spec, v_shape = rows(vcols), jax.ShapeDtypeStruct((r, vcols), BF16)
    else:
        per = per_batch_t // tm
        v_spec = pl.BlockSpec((None, H_MLA * VT_ROWS, tm), lambda i: (i // per, 0, i % per))
        v_shape = jax.ShapeDtypeStruct((r // per_batch_t, H_MLA * VT_ROWS, per_batch_t), BF16)
    return pl.pallas_call(
        functools.partial(_kvup_kernel, v_transposed=per_batch_t is not None),
        grid=(r // tm,),
        in_specs=[pl.BlockSpec((tm, D_LATENT), lambda i: (i + off, 0)),
                  _layer_spec(wuk, layer), _layer_spec(wuv, layer), full((1, D_NOPE))],
        out_specs=[rows(H_MLA * D_NOPE), v_spec],
        out_shape=(jax.ShapeDtypeStruct((r, H_MLA * D_NOPE), BF16), v_shape),
        compiler_params=_params("parallel"),
        name="kv_up",
    )(lat_all, wuk, wuv, gkn)


def _mla_cached_kernel(q_ref, knp_ref, kpep_ref, vp_ref, knn_ref, kpen_ref, vn_ref, o_ref,
                       m_scr, l_scr, acc_scr, *, t, tk, past):
    m_scr[...] = jnp.full(m_scr.shape, MASK_VALUE, F32)
    l_scr[...] = jnp.zeros(l_scr.shape, F32)
    acc_scr[...] = jnp.zeros(acc_scr.shape, F32)
    qk = D_NOPE + D_ROPE

    def scores(kn, kpe):
        ss = []
        for hd in range(H_MLA):
            kb = jnp.concatenate([kn[:, hd * D_NOPE:(hd + 1) * D_NOPE], kpe], axis=-1)
            ss.append(_dot_nt(kb, q_ref[:, hd * Q_HEAD_COLS:hd * Q_HEAD_COLS + qk]))
        return ss

    def values(ss, v, mask):
        for hd, s in enumerate(ss):
            if mask is not None:
                s = jnp.where(mask, s, MASK_VALUE)
            m_old = m_scr[hd]
            m_new = jnp.maximum(m_old, jnp.max(s, axis=0, keepdims=True))
            alpha = jnp.exp2(m_old - m_new)
            p = jnp.exp2(s - m_new)
            l_scr[hd] = alpha * l_scr[hd] + jnp.sum(p, axis=0, keepdims=True)
            acc_scr[hd] = alpha * acc_scr[hd] + _dot_tn(v[:, hd * D_VA:(hd + 1) * D_VA], p.astype(BF16))
            m_scr[hd] = m_new

    if t > CHUNK:
        kc = lax.broadcasted_iota(jnp.int32, (t, t), 0) // CHUNK
        qc = lax.broadcasted_iota(jnp.int32, (t, t), 1) // CHUNK
        mask = kc <= qc
    else:
        mask = None

    def cached(j):
        return slice(j * tk, (j + 1) * tk)

    nblk = past // tk
    ss = scores(knp_ref[cached(0), :], kpep_ref[cached(0), :].astype(BF16)) if nblk else None
    for j in range(nblk):
        if j + 1 < nblk:
            ss_next = scores(knp_ref[cached(j + 1), :], kpep_ref[cached(j + 1), :].astype(BF16))
        else:
            ss_next = scores(knn_ref[...], kpen_ref[:, 0:D_ROPE])
        values(ss, vp_ref[cached(j), :], None)
        ss = ss_next
    if not nblk:
        ss = scores(knn_ref[...], kpen_ref[:, 0:D_ROPE])
    values(ss, vn_ref[...], mask)

    for hd in range(H_MLA):
        o = acc_scr[hd] * (1.0 / l_scr[hd])
        o_ref[:, hd * D_VA:(hd + 1) * D_VA] = o.T


def _mla_attn_cached(q, knp, kpe_cache, vp, knn, kpen, vn, *, layer, batch, t, past, tk):
    assert past % tk == 0 and past % CHUNK == 0 and t % CHUNK == 0
    vcols = H_MLA * D_VA
    new = lambda a: pl.BlockSpec((t, a.shape[1]), lambda b: (b, 0))
    old = lambda a: pl.BlockSpec((past, a.shape[1]), lambda b: (b, 0))
    scratch = [pltpu.VMEM((H_MLA, 1, t), F32), pltpu.VMEM((H_MLA, 1, t), F32), pltpu.VMEM((H_MLA, D_VA, t), F32)]
    return pl.pallas_call(
        functools.partial(_mla_cached_kernel, t=t, tk=tk, past=past),
        grid=(batch,),
        in_specs=[new(q), old(knp),
                  pl.BlockSpec((None, None, past, D_ROPE), lambda b: (layer, b, 0, 0)),
                  old(vp), new(knn), new(kpen), new(vn)],
        out_specs=pl.BlockSpec((t, vcols), lambda b: (b, 0)),
        out_shape=jax.ShapeDtypeStruct((batch * t, vcols), F32),
        scratch_shapes=scratch,
        compiler_params=_params("parallel"),
        name="mla_attn_cached",
    )(q, knp, kpe_cache, vp, knn, kpen, vn)


def _mla_first_kernel(q_ref, kn_ref, kpe_ref, vt_ref, o_ref, qt_scr, m_scr, acc_scr, s_scr, *, tq, tk):
    i = pl.program_id(1)
    nb = tq // tk
    assert nb == 2
    for hd in range(H_MLA):
        qt_scr[hd] = q_ref[:, hd * Q_HEAD_COLS:(hd + 1) * Q_HEAD_COLS].T
    m_scr[...] = jnp.full(m_scr.shape, MASK_VALUE, F32)
    acc_scr[...] = jnp.zeros(acc_scr.shape, F32)
    last = i * nb + nb - 1

    def scores(b, buf):
        r0 = pl.multiple_of(b * tk, tk)
        kpe = kpe_ref[pl.ds(r0, tk), :]
        for hd in range(H_MLA):
            kb = jnp.concatenate([kn_ref[pl.ds(r0, tk), hd * D_NOPE:(hd + 1) * D_NOPE], kpe], axis=-1)
            s = _dot(kb, qt_scr[hd])
            for g in range(tq // LANES):
                s_scr[buf, hd, g] = s[:, g * LANES:(g + 1) * LANES]

    def values(b, buf, own):
        r0 = pl.multiple_of(b * tk, tk)
        for hd in range(H_MLA):
            vt = vt_ref[hd * VT_ROWS:(hd + 1) * VT_ROWS, pl.ds(r0, tk)]
            for w0 in range(0, tq, MXU_COLS):
                ps, alphas = [], []
                for c0 in range(w0, w0 + MXU_COLS, LANES):
                    qs = slice(c0, c0 + LANES)
                    s = s_scr[buf, hd, c0 // LANES]
                    if own is not None:
                        kc = lax.broadcasted_iota(jnp.int32, (tk, LANES), 0) // CHUNK + own * (tk // CHUNK)
                        qc = (lax.broadcasted_iota(jnp.int32, (tk, LANES), 1) + c0) // CHUNK
                        s = jnp.where(kc <= qc, s, MASK_VALUE)
                    m_old = m_scr[hd, :, qs]
                    m_new = jnp.maximum(m_old, jnp.max(s, axis=0, keepdims=True))
                    m_scr[hd, :, qs] = m_new
                    alphas.append(jnp.exp2(m_old - m_new))
                    ps.append(jnp.exp2(s - m_new).astype(BF16))
                w = w0 // MXU_COLS
                acc_scr[hd, w] = (jnp.concatenate(alphas, axis=1) * acc_scr[hd, w]
                                  + _dot(vt, jnp.concatenate(ps, axis=1)))

    def step(t, buf, own=None):
        scores(t, buf)
        values(t - 1, 1 - buf, own)

    scores(0, 0)

    def pair(k, carry):
        t = 1 + 2 * k
        step(t, 1)
        step(t + 1, 0)
        return carry

    def four_pairs(k, carry):
        for n in range(4):
            carry = pair(4 * k + n, carry)
        return carry

    lax.fori_loop(0, i // 4, four_pairs, 0)
    lax.fori_loop(i // 4 * 4, i, pair, 0)
    step(last, 1, own=0)
    values(last, 1, 1)

    for hd in range(H_MLA):
        for w in range(tq // MXU_COLS):
            o = acc_scr[hd, w, 0:D_VA, :] * (1.0 / acc_scr[hd, w, D_VA:D_VA + 1, :])
            o_ref[w * MXU_COLS:(w + 1) * MXU_COLS, hd * D_VA:(hd + 1) * D_VA] = o.T


def _mla_attn_first(q, kn, kpe, vt, *, batch, t, tq, tk):
    assert t % tq == 0 and tq == 2 * tk and tk % CHUNK == 0
    nq = t // tq
    vcols = H_MLA * D_VA
    resident = lambda a: pl.BlockSpec((t, a.shape[1]), lambda b, i: (b, 0), pipeline_mode=pl.Buffered(1))
    scratch = [pltpu.VMEM((H_MLA, Q_HEAD_COLS, tq), BF16),
               pltpu.VMEM((H_MLA, 1, tq), F32),
               pltpu.VMEM((H_MLA, tq // MXU_COLS, VT_ROWS, MXU_COLS), F32),
               pltpu.VMEM((2, H_MLA, tq // LANES, tk, LANES), F32)]
    return pl.pallas_call(
        functools.partial(_mla_first_kernel, tq=tq, tk=tk),
        grid=(batch, nq),
        in_specs=[pl.BlockSpec((tq, q.shape[1]), lambda b, i: (b * nq + i, 0)), resident(kn), resident(kpe),
                  pl.BlockSpec((None, H_MLA * VT_ROWS, t), lambda b, i: (b, 0, 0), pipeline_mode=pl.Buffered(1))],
        out_specs=pl.BlockSpec((tq, vcols), lambda b, i: (b * nq + i, 0)),
        out_shape=jax.ShapeDtypeStruct((batch * t, vcols), F32),
        scratch_shapes=scratch,
        compiler_params=_params("parallel", "arbitrary"),
        name="mla_attn_first",
    )(q, kn, kpe, vt)


def _sb_kernel(q_ref, *refs, tq, tsub, tk, past):
    if past:
        kp_ref, vp_ref, k_ref, v_ref, o_ref, c_scr, acc_scr = refs
    else:
        k_ref, v_ref, o_ref, c_scr, acc_scr = refs
    i = pl.program_id(1)
    nsub = tq // tsub
    c_scr[...] = jnp.zeros(c_scr.shape, F32)
    acc_scr[...] = jnp.zeros(acc_scr.shape, F32)
    heads = [slice(hd * D_SB, (hd + 1) * D_SB) for hd in range(H_SB)]
    chains = [(u, hd) for u in range(nsub) for hd in range(H_SB)]

    def later_eq(n):
        return (lax.broadcasted_iota(jnp.int32, (n, n), 1)
                >= lax.broadcasted_iota(jnp.int32, (n, n), 0)).astype(BF16)

    def new_rows(r0, n):
        return ([k_ref[pl.ds(r0, n), cols] for cols in heads], [v_ref[pl.ds(r0, n), cols] for cols in heads])

    def cached_rows(r0, n):
        return ([kp_ref[pl.ds(r0, n), hd, :].astype(BF16) for hd in range(H_SB)],
                [vp_ref[pl.ds(r0, n), hd, :].astype(BF16) for hd in range(H_SB)])

    def update(kvs, live, tri, before):
        zs = [_dot_nt(kvs[u][0][hd], q_ref[u * tsub:(u + 1) * tsub, heads[hd]]) for u, hd in chains]
        es, incls = [], []
        for z in zs:
            ls = jnp.minimum(-z, 0.0) - jnp.log2(1.0 + jnp.exp2(-jnp.abs(z)))
            lsm = ls if before is None else jnp.where(before, ls, 0.0)
            incl = _dot(tri, lsm.astype(BF16))
            incls.append(incl)
            es.append(z + incl if before is None else z + ls + (incl - lsm))
        top = None
        for n, (u, hd) in enumerate(chains):
            c = c_scr[u, hd]
            e = es[n] + c
            if live[u] is not None:
                e = jnp.where(live[u], e, MASK_VALUE)
            w = jnp.exp2(e)
            if before is not None:
                w = jnp.where(before, w, 0.0)
            acc_scr[u, hd] += _dot_tn(kvs[u][1][hd], w.astype(BF16))
            c_new = c + incls[n][0:1, :]
            c_scr[u, hd] = c_new
            top_c = jnp.max(c_new)
            if live[u] is not None:
                top_c = jnp.where(live[u], top_c, MASK_VALUE)
            top = top_c if top is None else jnp.maximum(top, top_c)
        return top

    own0 = i * tq
    before = (lax.broadcasted_iota(jnp.int32, (tsub, tsub), 0)
              < lax.broadcasted_iota(jnp.int32, (tsub, tsub), 1))
    top = update([new_rows(pl.multiple_of(own0 + u * tsub, tsub), tsub) for u in range(nsub)],
                 [None] * nsub, later_eq(tsub), before)
    tri_k = later_eq(tk)
    older = cached_rows if past else new_rows
    newest = [(past + own0 + u * tsub) // tk - 1 for u in range(nsub)]

    def cond(st):
        k, top = st
        return jnp.logical_and(newest[-1] - k >= 0, top > SB_EXP2_ZERO)

    def body(st):
        k, _ = st
        kvs, live = [], []
        for u in range(nsub):
            j = newest[u] - k
            kvs.append(older(pl.multiple_of(jnp.maximum(j, 0) * tk, tk), tk))
            live.append(None if u == nsub - 1 else j >= 0)
        return k + 1, update(kvs, live, tri_k, None)

    lax.while_loop(cond, body, (0, top))
    for u, hd in chains:
        o_ref[u * tsub:(u + 1) * tsub, heads[hd]] = acc_scr[u, hd].T


def _sb_attn(q, k, v, *, batch, t, tq, tk, past=0, k_cache=None, v_cache=None, layer=0):
    tsub = tq if past else tk
    assert t % tq == 0 and past % tk == 0 and tq % tsub == 0 and (past == 0 or t == tq)
    nq = t // tq
    nsub = tq // tsub
    cols = H_SB * D_SB
    q_spec = pl.BlockSpec((tq, cols), lambda b, i: (b * nq + i, 0))
    if past:
        cache = pl.BlockSpec((None, None, past, H_SB, D_SB), lambda b, i: (layer, b, 0, 0, 0))
        new = pl.BlockSpec((t, cols), lambda b, i: (b, 0))
        in_specs, operands = [q_spec, cache, cache, new, new], (q, k_cache, v_cache, k, v)
    else:
        resident = pl.BlockSpec((t, cols), lambda b, i: (b, 0), pipeline_mode=pl.Buffered(1))
        in_specs, operands = [q_spec, resident, resident], (q, k, v)
    return pl.pallas_call(
        functools.partial(_sb_kernel, tq=tq, tsub=tsub, tk=tk, past=past),
        grid=(batch, nq),
        in_specs=in_specs,
        out_specs=q_spec,
        out_shape=jax.ShapeDtypeStruct((batch * t, cols), F32),
        scratch_shapes=[pltpu.VMEM((nsub, H_SB, 1, tsub), F32), pltpu.VMEM((nsub, H_SB, D_SB, tsub), F32)],
        compiler_params=_params("parallel", "arbitrary"),
        name="sb_attn",
    )(*operands)


def _post_kernel(oa_ref, ob_ref, x_ref, mod_ref, goa_ref, gob_ref, gnf_ref, wout_ref, wg_ref, wu_ref, wd_ref,
                 wconv_ref, bconv_ref, cbuf_ref, y_ref, nconv_ref, x1_scr, h_scr, u_scr,
                 *, seg, nseg, d, per_batch):
    i = pl.program_id(0)
    na = oa_ref.shape[1]
    ma = _rms_rows(oa_ref[...], goa_ref[...]).astype(BF16)
    mb = _rms_rows(ob_ref[...], gob_ref[...]).astype(BF16)
    mix = _dot(ma, wout_ref[0:na, :]) + _dot(mb, wout_ref[na:, :])
    for s in range(nseg):
        rows = slice(s * seg, (s + 1) * seg)
        m = mod_ref[s]
        x1 = x_ref[rows, :] + m[:, 2 * d:3 * d] * mix[rows, :]
        x1_scr[rows, :] = x1
        h = _rms_rows(x1, gnf_ref[...]) * (1.0 + m[:, 4 * d:5 * d]) + m[:, 3 * d:4 * d]
        h_scr[rows, :] = h.astype(BF16)
    h = h_scr[...]

    if nseg == 1:
        first = i % per_batch == 0

        @pl.when(first)
        def _():
            u_scr[0, 8 - (CONV_W - 1):8, :] = cbuf_ref[0]

        @pl.when(jnp.logical_not(first))
        def _():
            u_scr[0, 0:8, :] = u_scr[0, seg:seg + 8, :]
    else:
        for s in range(nseg):
            u_scr[s, 8 - (CONV_W - 1):8, :] = cbuf_ref[s]

    ff = wg_ref.shape[1]
    bounds = [ff * c // FF_CHUNKS // MXU_COLS * MXU_COLS for c in range(FF_CHUNKS)] + [ff]
    chunks = [slice(bounds[c], bounds[c + 1]) for c in range(FF_CHUNKS)]
    us = [_dot(h, wg_ref[:, c]) for c in chunks]
    ups = [_dot(h, wu_ref[:, c]) for c in chunks]
    down = None
    for c, u, up in zip(chunks, us, ups):
        for s in range(nseg):
            u_scr[s, 8:8 + seg, c] = u[s * seg:(s + 1) * seg, :]
            nconv_ref[s, :, c] = u[(s + 1) * seg - (CONV_W - 1):(s + 1) * seg, :]
        parts = []
        for s in range(nseg):
            uc = bconv_ref[:, c]
            for tap in range(CONV_W):
                lo = 8 - (CONV_W - 1) + tap
                uc = uc + u_scr[s, lo:lo + seg, c] * wconv_ref[tap:tap + 1, c]
            parts.append(uc)
        uc = parts[0] if nseg == 1 else jnp.concatenate(parts, axis=0)
        act = (uc * jax.nn.sigmoid(uc) * up).astype(BF16)
        part = _dot(act, wd_ref[c, :])
        down = part if down is None else down + part
    for s in range(nseg):
        rows = slice(s * seg, (s + 1) * seg)
        y_ref[rows, :] = x1_scr[rows, :] + mod_ref[s][:, 5 * d:6 * d] * down[rows, :]


def _post(oa, ob, x2, mod3, goa, gob, gnf, wout, wg, wu, wd, wconv, bconv, cbuf, *, layer, batch, t, tm_max):
    r, d = x2.shape
    ff = wg.shape[-1]
    tm, seg, nseg = _row_tiling(batch, t, tm_max)
    per_batch = max(1, t // tm)
    full = lambda shape: pl.BlockSpec(shape, lambda i: (0,) * len(shape))
    rows = lambda w: pl.BlockSpec((tm, w), lambda i: (i, 0))
    if nseg == 1:
        state_spec = pl.BlockSpec((1, CONV_W - 1, ff), lambda i: (i // per_batch, 0, 0))
    else:
        state_spec = pl.BlockSpec((nseg, CONV_W - 1, ff), lambda i: (i, 0, 0))
    return pl.pallas_call(
        functools.partial(_post_kernel, seg=seg, nseg=nseg, d=d, per_batch=per_batch),
        grid=(r // tm,),
        in_specs=[rows(oa.shape[1]), rows(ob.shape[1]), rows(d), _mod_spec(t, tm, nseg, 6 * d),
                  full((1, oa.shape[1])), full((1, ob.shape[1])), full((1, d)),
                  _layer_spec(wout, layer), _layer_spec(wg, layer), _layer_spec(wu, layer),
                  _layer_spec(wd, layer),
                  full((CONV_W, ff)), full((1, ff)), state_spec],
        out_specs=[rows(d), state_spec],
        out_shape=(jax.ShapeDtypeStruct((r, d), F32),
                   jax.ShapeDtypeStruct((batch, CONV_W - 1, ff), F32)),
        scratch_shapes=[pltpu.VMEM((tm, d), F32), pltpu.VMEM((tm, d), BF16),
                        pltpu.VMEM((nseg, seg + 8, ff), F32)],
        compiler_params=_params("arbitrary"),
        name="post_ffn",
    )(oa, ob, x2, mod3, goa, gob, gnf, wout, wg, wu, wd, wconv, bconv, cbuf)


def _in_col_runs():
    qa = H_MLA * (D_NOPE + D_ROPE)
    half = D_ROPE // 2
    runs = []
    for h in range(H_MLA):
        base = h * (D_NOPE + D_ROPE)
        pe = base + D_NOPE
        runs += [(base, pe + D_ROPE), (pe + half, pe + D_ROPE), (pe, pe + half)]
    runs.append((qa, qa + D_LATENT))
    kp = qa + D_LATENT
    runs += [(kp, kp + D_ROPE), (kp + half, kp + D_ROPE), (kp, kp + half)]
    runs.append((kp + D_ROPE, kp + D_ROPE + 3 * H_SB * D_SB))
    return runs


def _win_kernel(w_ref, o_ref):
    w = w_ref[...]
    o_ref[...] = jnp.concatenate([w[:, a:b] for a, b in _in_col_runs()], axis=1).astype(o_ref.dtype)


def _permute_w_in(w_in):
    depth, d, n = w_in.shape
    x = w_in.reshape(depth * d, n)
    tm = 256
    out = pl.pallas_call(
        _win_kernel,
        grid=(depth * d // tm,),
        in_specs=[pl.BlockSpec((tm, n), lambda i: (i, 0))],
        out_specs=pl.BlockSpec((tm, C_END), lambda i: (i, 0)),
        out_shape=jax.ShapeDtypeStruct((depth * d, C_END), BF16),
        compiler_params=_params("parallel"),
        name="permute_w_in",
    )(x)
    return out.reshape(depth, d, C_END)


def _rope_gain(g):
    half = D_ROPE // 2
    return jnp.concatenate([g, g[..., half:], g[..., :half]], axis=-1)


def _rope_tables(pos0, t):
    half = D_ROPE // 2
    freqs = ROPE_THETA ** (-jnp.arange(half, dtype=F32) / half)

    def direct(pos):
        ang = pos.astype(F32)[:, None] * freqs[None, :]
        return jnp.cos(ang), jnp.sin(ang)

    if t % LANES == 0 and pos0 % LANES == 0 and t > LANES:
        ca, sa = direct(pos0 + LANES * jnp.arange(t // LANES, dtype=jnp.int32))
        cb, sb = direct(jnp.arange(LANES, dtype=jnp.int32))
        cos = (ca[:, None] * cb[None] - sa[:, None] * sb[None]).reshape(t, half)
        sin = (sa[:, None] * cb[None] + ca[:, None] * sb[None]).reshape(t, half)
    else:
        cos, sin = direct(pos0 + jnp.arange(t, dtype=jnp.int32))
    return jnp.tile(cos, (1, 4)), jnp.concatenate([-sin, sin, sin, -sin], axis=-1)


def kernel(x_prompt, x_sample, c_prompt, c_sample, cache_mla_latent, cache_mla_krope, cache_sb_k, cache_sb_v, state_ffn_conv, w_ada, b_ada, g_norm_mix, g_norm_ffn, w_in, g_kv_latent, g_q_nope, g_q_rope, g_k_nope, g_k_rope, w_uk, w_uv, g_out_mla, g_out_sb, w_out, w_gate, w_up, w_conv, b_conv, w_down):
    depth = w_in.shape[0]
    bp, tp, d = x_prompt.shape
    bs, ts, _ = x_sample.shape
    past = cache_mla_latent.shape[2]
    ff = w_gate.shape[2]
    sb = H_SB * D_SB

    nb = bp + bs
    nb_pad = -(-nb // 8) * 8
    c_all = jnp.concatenate([c_prompt, c_sample, jnp.zeros((nb_pad - nb, d), F32)], axis=0)
    mod = _ada_all(c_all, w_ada, b_ada)

    w_in_p = _permute_w_in(w_in)
    w_uk16, w_uv16, w_out16 = _to_bf16(w_uk), _to_bf16(w_uv), _to_bf16(w_out)
    w_uvt16 = jnp.swapaxes(w_uv16, 1, 2)
    w_gate16, w_up16, w_down16 = _to_bf16(w_gate), _to_bf16(w_up), _to_bf16(w_down)
    gqr, gkr = _rope_gain(g_q_rope), _rope_gain(g_k_rope)

    cos_p, sin_p = _rope_tables(0, tp)
    cos_p, sin_p = jnp.tile(cos_p, (bp, 1)), jnp.tile(sin_p, (bp, 1))
    cos_s, sin_s = _rope_tables(past, ts)
    cos_s, sin_s = jnp.tile(cos_s, (bs, 1)), jnp.tile(sin_s, (bs, 1))

    lat_cache_rows = cache_mla_latent.reshape(-1, D_LATENT)

    def layer(l, x2, mod3, cos, sin, batch, t, p, stacks):
        row = lambda a: a[l][None, :]
        qm, lat_st, kpe_st, kpe128, qb, kb_st, vb_st, kb16, vb16 = _inproj(
            x2, mod3, row(g_norm_mix), w_in_p, row(g_q_nope), row(gqr), row(g_kv_latent), row(gkr),
            cos, sin, batch=batch, t=t, tm_max=512, layer=l, depth=depth, stacks=stacks)
        new_lat = dict(row0=l * batch * t, n_rows=batch * t)
        lat_rows = lat_st.reshape(-1, D_LATENT)
        if p:
            cbuf = state_ffn_conv[l]
            knp, vp = _kvup(lat_cache_rows, w_uk16, w_uv16, row(g_k_nope), layer=l, tm_max=2048,
                            row0=l * batch * p, n_rows=batch * p)
            knn, vn = _kvup(lat_rows, w_uk16, w_uv16, row(g_k_nope), layer=l, tm_max=512, **new_lat)
            oa = _mla_attn_cached(qm, knp, cache_mla_krope, vp, knn, kpe128, vn,
                                  layer=l, batch=batch, t=t, past=p, tk=256)
            ob = _sb_attn(qb, kb16, vb16, batch=batch, t=t, tq=t, tk=256, past=p,
                          k_cache=cache_sb_k, v_cache=cache_sb_v, layer=l)
        else:
            cbuf = jnp.zeros((batch, CONV_W - 1, ff), F32)
            kn, vt = _kvup(lat_rows, w_uk16, w_uvt16, row(g_k_nope), layer=l, tm_max=2048, per_batch_t=t,
                           **new_lat)
            oa = _mla_attn_first(qm, kn, kpe128, vt, batch=batch, t=t, tq=512, tk=256)
            ob = _sb_attn(qb, kb16, vb16, batch=batch, t=t, tq=512, tk=256)
        y, nconv = _post(oa, ob, x2, mod3, row(g_out_mla), row(g_out_sb), row(g_norm_ffn), w_out16,
                         w_gate16, w_up16, w_down16, w_conv[l], row(b_conv), cbuf,
                         layer=l, batch=batch, t=t, tm_max=512)
        return y, (lat_st, kpe_st, kb_st, vb_st), nconv

    xp = x_prompt.reshape(bp * tp, d)
    xs = x_sample.reshape(bs * ts, d)
    st_p = st_s = None
    conv_p, conv_s = [], []
    for l in range(depth):
        mod_p = mod[l, :bp].reshape(bp, 1, 6 * d)
        mod_s = mod[l, bp:nb].reshape(bs, 1, 6 * d)
        xp, st_p, nc_p = layer(l, xp, mod_p, cos_p, sin_p, bp, tp, 0, st_p)
        xs, st_s, nc_s = layer(l, xs, mod_s, cos_s, sin_s, bs, ts, past, st_s)
        conv_p.append(nc_p)
        conv_s.append(nc_s)

    def step_outputs(stacks, convs, batch, t):
        lat_st, kpe_st, kb_st, vb_st = stacks
        return (lat_st.reshape(depth, batch, t, D_LATENT), kpe_st.reshape(depth, batch, t, D_ROPE),
                kb_st.reshape(depth, batch, t, H_SB, D_SB), vb_st.reshape(depth, batch, t, H_SB, D_SB),
                jnp.stack(convs))

    return ((xp.reshape(bp, tp, d), xs.reshape(bs, ts, d))
            + step_outputs(st_p, conv_p, bp, tp) + step_outputs(st_s, conv_s, bs, ts))
```

```python
import functools
import math

import jax
import jax.numpy as jnp
from jax import lax
from jax.experimental import pallas as pl
from jax.experimental.pallas import tpu as pltpu

F32 = jnp.float32
BF16 = jnp.bfloat16

H_MLA = 4
D_NOPE = 128
D_ROPE = 64
D_VA = 128
D_LATENT = 512
H_SB = 4
D_SB = 128
CONV_W = 3
CHUNK = 64
ROPE_THETA = 10000.0
EPS = 1e-6

LANES = 128
MXU_COLS = 256
Q_HEAD_COLS = 2 * LANES
MASK_VALUE = -1e30
FF_CHUNKS = 2
ONES_ROWS = 16
VT_ROWS = D_VA + ONES_ROWS
SB_EXP2_ZERO = -152.0
VMEM_LIMIT_BYTES = 56 * 1024 * 1024

MLA_Q_SCALE = math.log2(math.e) * (D_NOPE + D_ROPE) ** -0.5
SB_Q_SCALE = math.log2(math.e) * D_SB ** -0.5


def _rms_rows(x, g):
    return x * lax.rsqrt(jnp.mean(x * x, axis=-1, keepdims=True) + EPS) * g


def _dot(a, b):
    return jnp.dot(a, b, preferred_element_type=F32)


def _dot_nt(a, b):
    return lax.dot_general(a, b, (((1,), (1,)), ((), ())), preferred_element_type=F32)


def _dot_tn(a, b):
    return lax.dot_general(a, b, (((0,), (0,)), ((), ())), preferred_element_type=F32)


def _layer_spec(w, layer):
    return pl.BlockSpec((None,) + w.shape[1:], lambda *_: (layer, 0, 0), pipeline_mode=pl.Buffered(1))


def _params(*sem):
    return pltpu.CompilerParams(dimension_semantics=sem, vmem_limit_bytes=VMEM_LIMIT_BYTES)


def _cast_kernel(x_ref, o_ref):
    o_ref[...] = x_ref[...].astype(o_ref.dtype)


def _to_bf16(w):
    x = w.reshape(-1, w.shape[-1])
    r, c = x.shape
    tm = 512
    while r % tm:
        tm //= 2
    out = pl.pallas_call(
        _cast_kernel,
        grid=(r // tm,),
        in_specs=[pl.BlockSpec((tm, c), lambda i: (i, 0))],
        out_specs=pl.BlockSpec((tm, c), lambda i: (i, 0)),
        out_shape=jax.ShapeDtypeStruct((r, c), BF16),
        compiler_params=_params("parallel"),
        name="cast_bf16",
    )(x)
    return out.reshape(w.shape)


def _ada_kernel(c_ref, w_ref, b_ref, o_ref):
    c = c_ref[...]
    s = (c * jax.nn.sigmoid(c)).astype(BF16)
    o_ref[0] = _dot(s, w_ref[0].astype(BF16)) + b_ref[0]


def _ada_all(c_all, w_ada, b_ada):
    depth, d, n = w_ada.shape
    bp = c_all.shape[0]
    tn = d
    return pl.pallas_call(
        _ada_kernel,
        grid=(depth, n // tn),
        in_specs=[
            pl.BlockSpec((bp, d), lambda l, j: (0, 0)),
            pl.BlockSpec((1, d, tn), lambda l, j: (l, 0, j)),
            pl.BlockSpec((1, 1, tn), lambda l, j: (l, 0, j)),
        ],
        out_specs=pl.BlockSpec((1, bp, tn), lambda l, j: (l, 0, j)),
        out_shape=jax.ShapeDtypeStruct((depth, bp, n), F32),
        compiler_params=_params("parallel", "parallel"),
        name="ada_mod",
    )(c_all, w_ada, b_ada.reshape(depth, 1, n))


C_QM = 0
C_LAT = H_MLA * Q_HEAD_COLS
C_KPE = C_LAT + D_LATENT
C_QB = C_KPE + LANES
C_KB = C_QB + H_SB * D_SB
C_VB = C_KB + H_SB * D_SB
C_END = C_VB + H_SB * D_SB


def _rope_pair(v, g, cos, sin):
    ssq = jnp.sum(v * v, axis=-1, keepdims=True) * (0.5 / D_ROPE)
    vn = v * lax.rsqrt(ssq + EPS) * g
    return vn * cos + pltpu.roll(vn, D_ROPE, 1) * sin


N_INPROJ_IN = 10


def _inproj_kernel(*refs, seg, nseg, d):
    x_ref, mod_ref, gn_ref, w_ref, gqn_ref, gqr_ref, glat_ref, gkr_ref, cos_ref, sin_ref = refs[:N_INPROJ_IN]
    qm_ref, lat_ref, kpe_ref, kpe128_ref, qb_ref, kb_ref, vb_ref, kb16_ref, vb16_ref, h_scr = refs[-10:]
    for s in range(nseg):
        rows = slice(s * seg, (s + 1) * seg)
        m = mod_ref[s]
        h = _rms_rows(x_ref[rows, :], gn_ref[...]) * (1.0 + m[:, d:2 * d]) + m[:, 0:d]
        h_scr[rows, :] = h.astype(BF16)
    h = h_scr[...]
    cos = cos_ref[...]
    sin = sin_ref[...]

    for hd in range(H_MLA):
        c0 = C_QM + hd * Q_HEAD_COLS
        qh = _dot(h, w_ref[:, c0:c0 + Q_HEAD_COLS])
        qn = _rms_rows(qh[:, :D_NOPE], gqn_ref[...]) * MLA_Q_SCALE
        qp = _rope_pair(qh[:, D_NOPE:], gqr_ref[...], cos, sin) * MLA_Q_SCALE
        qm_ref[:, c0:c0 + D_NOPE] = qn.astype(BF16)
        qm_ref[:, c0 + D_NOPE:c0 + Q_HEAD_COLS] = qp.astype(BF16)

    lat_ref[...] = _rms_rows(_dot(h, w_ref[:, C_LAT:C_KPE]), glat_ref[...])

    kp = _rope_pair(_dot(h, w_ref[:, C_KPE:C_QB]), gkr_ref[...], cos, sin)
    kpe_ref[...] = kp[:, :D_ROPE]
    lane = lax.broadcasted_iota(jnp.int32, kp.shape, 1)
    kpe128_ref[...] = jnp.where(lane < D_ROPE, kp, 0.0).astype(BF16)

    qb_ref[...] = (_dot(h, w_ref[:, C_QB:C_KB]) * SB_Q_SCALE).astype(BF16)
    n = h.shape[0]
    kb = _dot(h, w_ref[:, C_KB:C_VB])
    kb16_ref[...] = kb.astype(BF16)
    vb = _dot(h, w_ref[:, C_VB:C_END])
    vb16_ref[...] = vb.astype(BF16)
    for hd in range(H_SB):
        head_rows = pl.ds(hd, n, stride=H_SB)
        kb_ref[head_rows, :] = kb[:, hd * D_SB:(hd + 1) * D_SB]
        vb_ref[head_rows, :] = vb[:, hd * D_SB:(hd + 1) * D_SB]


def _row_tiling(batch, t, tm_max):
    if t >= tm_max:
        assert t % tm_max == 0
        return tm_max, tm_max, 1
    nseg = max(1, min(batch, tm_max // t))
    while batch % nseg:
        nseg -= 1
    return nseg * t, t, nseg


def _mod_spec(t, tm, nseg, width):
    if nseg == 1:
        per = t // tm
        return pl.BlockSpec((1, 1, width), lambda i: (i // per, 0, 0))
    return pl.BlockSpec((nseg, 1, width), lambda i: (i, 0, 0))


STACKED_OUTS = (1, 2, 5, 6)


def _inproj(x2, mod3, gn, w_p, gqn, gqr, glat, gkr, cos, sin, *, batch, t, tm_max, layer, depth, stacks):
    r, d = x2.shape
    tm, seg, nseg = _row_tiling(batch, t, tm_max)
    full = lambda shape: pl.BlockSpec(shape, lambda i: (0,) * len(shape))
    rows = lambda w: pl.BlockSpec((tm, w), lambda i: (i, 0))
    sb = H_SB * D_SB
    outs = ((1, H_MLA * Q_HEAD_COLS, BF16),
            (1, D_LATENT, F32),
            (1, D_ROPE, F32),
            (1, LANES, BF16),
            (1, sb, BF16),
            (H_SB, D_SB, F32), (H_SB, D_SB, F32),
            (1, sb, BF16), (1, sb, BF16))
    out_shapes, out_specs = [], []
    for k, (per, w, dt) in enumerate(outs):
        if k in STACKED_OUTS:
            out_shapes.append(jax.ShapeDtypeStruct((depth, r * per, w), dt))
            out_specs.append(pl.BlockSpec((None, tm * per, w), lambda i: (layer, i, 0)))
        else:
            out_shapes.append(jax.ShapeDtypeStruct((r, w), dt))
            out_specs.append(rows(w))
    in_specs = [rows(d), _mod_spec(t, tm, nseg, 6 * d), full((1, d)), _layer_spec(w_p, layer),
                full((1, D_NOPE)), full((1, LANES)), full((1, D_LATENT)), full((1, LANES)),
                rows(LANES), rows(LANES)]
    operands = (x2, mod3, gn, w_p, gqn, gqr, glat, gkr, cos, sin)
    assert len(operands) == N_INPROJ_IN
    aliases = {}
    if stacks is not None:
        in_specs += [pl.BlockSpec(memory_space=pl.ANY)] * len(STACKED_OUTS)
        operands += tuple(stacks)
        aliases = {N_INPROJ_IN + n: k for n, k in enumerate(STACKED_OUTS)}
    return pl.pallas_call(
        functools.partial(_inproj_kernel, seg=seg, nseg=nseg, d=d),
        grid=(r // tm,),
        in_specs=in_specs,
        out_specs=out_specs,
        out_shape=tuple(out_shapes),
        input_output_aliases=aliases,
        scratch_shapes=[pltpu.VMEM((tm, d), BF16)],
        compiler_params=_params("parallel"),
        name="in_proj",
    )(*operands)


def _kvup_kernel(lat_ref, wuk_ref, wuv_ref, gkn_ref, k_ref, v_ref, *, v_transposed):
    l16 = lat_ref[...].astype(BF16)
    kn = _dot(l16, wuk_ref[...])
    for hd in range(H_MLA):
        cols = slice(hd * D_NOPE, (hd + 1) * D_NOPE)
        k_ref[:, cols] = _rms_rows(kn[:, cols], gkn_ref[...]).astype(BF16)
    if v_transposed:
        vt = _dot_nt(wuv_ref[...], l16).astype(BF16)
        ones = jnp.ones((ONES_ROWS, vt.shape[1]), BF16)
        for hd in range(H_MLA):
            v_ref[hd * VT_ROWS:hd * VT_ROWS + D_VA, :] = vt[hd * D_VA:(hd + 1) * D_VA, :]
            v_ref[hd * VT_ROWS + D_VA:(hd + 1) * VT_ROWS, :] = ones
    else:
        v_ref[...] = _dot(l16, wuv_ref[...]).astype(BF16)


def _kvup(lat_all, wuk, wuv, gkn, *, layer, tm_max, per_batch_t=None, row0=0, n_rows=None):
    r = lat_all.shape[0] if n_rows is None else n_rows
    tm = min(tm_max, r if per_batch_t is None else per_batch_t)
    while r % tm or row0 % tm:
        tm //= 2
    off = row0 // tm
    full = lambda shape: pl.BlockSpec(shape, lambda i: (0,) * len(shape))
    rows = lambda w: pl.BlockSpec((tm, w), lambda i: (i, 0))
    vcols = H_MLA * D_VA
    if per_batch_t is None:
        v_spec, v_shape = rows(vcols), jax.ShapeDtypeStruct((r, vcols), BF16)
    else:
        per = per_batch_t // tm
        v_spec = pl.BlockSpec((None, H_MLA * VT_ROWS, tm), lambda i: (i // per, 0, i % per))
        v_shape = jax.ShapeDtypeStruct((r // per_batch_t, H_MLA * VT_ROWS, per_batch_t), BF16)
    return pl.pallas_call(
        functools.partial(_kvup_kernel, v_transposed=per_batch_t is not None),
        grid=(r // tm,),
        in_specs=[pl.BlockSpec((tm, D_LATENT), lambda i: (i + off, 0)),
                  _layer_spec(wuk, layer), _layer_spec(wuv, layer), full((1, D_NOPE))],
        out_specs=[rows(H_MLA * D_NOPE), v_spec],
        out_shape=(jax.ShapeDtypeStruct((r, H_MLA * D_NOPE), BF16), v_shape),
        compiler_params=_params("parallel"),
        name="kv_up",
    )(lat_all, wuk, wuv, gkn)


def _mla_cached_kernel(q_ref, knp_ref, kpep_ref, vp_ref, knn_ref, kpen_ref, vn_ref, o_ref,
                       m_scr, l_scr, acc_scr, *, t, tk, past):
    m_scr[...] = jnp.full(m_scr.shape, MASK_VALUE, F32)
    l_scr[...] = jnp.zeros(l_scr.shape, F32)
    acc_scr[...] = jnp.zeros(acc_scr.shape, F32)
    qk = D_NOPE + D_ROPE

    def scores(kn, kpe):
        ss = []
        for hd in range(H_MLA):
            kb = jnp.concatenate([kn[:, hd * D_NOPE:(hd + 1) * D_NOPE], kpe], axis=-1)
            ss.append(_dot_nt(kb, q_ref[:, hd * Q_HEAD_COLS:hd * Q_HEAD_COLS + qk]))
        return ss

    def values(ss, v, mask):
        for hd, s in enumerate(ss):
            if mask is not None:
                s = jnp.where(mask, s, MASK_VALUE)
            m_old = m_scr[hd]
            m_new = jnp.maximum(m_old, jnp.max(s, axis=0, keepdims=True))
            alpha = jnp.exp2(m_old - m_new)
            p = jnp.exp2(s - m_new)
            l_scr[hd] = alpha * l_scr[hd] + jnp.sum(p, axis=0, keepdims=True)
            acc_scr[hd] = alpha * acc_scr[hd] + _dot_tn(v[:, hd * D_VA:(hd + 1) * D_VA], p.astype(BF16))
            m_scr[hd] = m_new

    if t > CHUNK:
        kc = lax.broadcasted_iota(jnp.int32, (t, t), 0) // CHUNK
        qc = lax.broadcasted_iota(jnp.int32, (t, t), 1) // CHUNK
        mask = kc <= qc
    else:
        mask = None

    def cached(j):
        return slice(j * tk, (j + 1) * tk)

    nblk = past // tk
    ss = scores(knp_ref[cached(0), :], kpep_ref[cached(0), :].astype(BF16)) if nblk else None
    for j in range(nblk):
        if j + 1 < nblk:
            ss_next = scores(knp_ref[cached(j + 1), :], kpep_ref[cached(j + 1), :].astype(BF16))
        else:
            ss_next = scores(knn_ref[...], kpen_ref[:, 0:D_ROPE])
        values(ss, vp_ref[cached(j), :], None)
        ss = ss_next
    if not nblk:
        ss = scores(knn_ref[...], kpen_ref[:, 0:D_ROPE])
    values(ss, vn_ref[...], mask)

    for hd in range(H_MLA):
        o = acc_scr[hd] * (1.0 / l_scr[hd])
        o_ref[:, hd * D_VA:(hd + 1) * D_VA] = o.T


def _mla_attn_cached(q, knp, kpe_cache, vp, knn, kpen, vn, *, layer, batch, t, past, tk):
    assert past % tk == 0 and past % CHUNK == 0 and t % CHUNK == 0
    vcols = H_MLA * D_VA
    new = lambda a: pl.BlockSpec((t, a.shape[1]), lambda b: (b, 0))
    old = lambda a: pl.BlockSpec((past, a.shape[1]), lambda b: (b, 0))
    scratch = [pltpu.VMEM((H_MLA, 1, t), F32), pltpu.VMEM((H_MLA, 1, t), F32), pltpu.VMEM((H_MLA, D_VA, t), F32)]
    return pl.pallas_call(
        functools.partial(_mla_cached_kernel, t=t, tk=tk, past=past),
        grid=(batch,),
        in_specs=[new(q), old(knp),
                  pl.BlockSpec((None, None, past, D_ROPE), lambda b: (layer, b, 0, 0)),
                  old(vp), new(knn), new(kpen), new(vn)],
        out_specs=pl.BlockSpec((t, vcols), lambda b: (b, 0)),
        out_shape=jax.ShapeDtypeStruct((batch * t, vcols), F32),
        scratch_shapes=scratch,
        compiler_params=_params("parallel"),
        name="mla_attn_cached",
    )(q, knp, kpe_cache, vp, knn, kpen, vn)


def _mla_first_kernel(q_ref, qnext_ref, kn_ref, kpe_ref, vt_ref, o_ref, qt_scr, m_scr, acc_scr, s_scr, *, tq, tk):
    i = pl.program_id(1)
    nb = tq // tk
    assert nb == 2

    def load_queries(ref):
        for hd in range(H_MLA):
            qt_scr[hd] = ref[:, hd * Q_HEAD_COLS:(hd + 1) * Q_HEAD_COLS].T

    m_scr[...] = jnp.full(m_scr.shape, MASK_VALUE, F32)
    acc_scr[...] = jnp.zeros(acc_scr.shape, F32)
    last = i * nb + nb - 1

    def scores(b, buf):
        r0 = pl.multiple_of(b * tk, tk)
        kpe = kpe_ref[pl.ds(r0, tk), :]
        for hd in range(H_MLA):
            kb = jnp.concatenate([kn_ref[pl.ds(r0, tk), hd * D_NOPE:(hd + 1) * D_NOPE], kpe], axis=-1)
            s = _dot(kb, qt_scr[hd])
            for g in range(tq // LANES):
                s_scr[buf, hd, g] = s[:, g * LANES:(g + 1) * LANES]

    def values(b, buf, own):
        r0 = pl.multiple_of(b * tk, tk)
        for hd in range(H_MLA):
            vt = vt_ref[hd * VT_ROWS:(hd + 1) * VT_ROWS, pl.ds(r0, tk)]
            for w0 in range(0, tq, MXU_COLS):
                ps, alphas = [], []
                for c0 in range(w0, w0 + MXU_COLS, LANES):
                    qs = slice(c0, c0 + LANES)
                    s = s_scr[buf, hd, c0 // LANES]
                    if own is not None:
                        kc = lax.broadcasted_iota(jnp.int32, (tk, LANES), 0) // CHUNK + own * (tk // CHUNK)
                        qc = (lax.broadcasted_iota(jnp.int32, (tk, LANES), 1) + c0) // CHUNK
                        s = jnp.where(kc <= qc, s, MASK_VALUE)
                    m_old = m_scr[hd, :, qs]
                    m_new = jnp.maximum(m_old, jnp.max(s, axis=0, keepdims=True))
                    m_scr[hd, :, qs] = m_new
                    alphas.append(jnp.exp2(m_old - m_new))
                    ps.append(jnp.exp2(s - m_new).astype(BF16))
                w = w0 // MXU_COLS
                acc_scr[hd, w] = (jnp.concatenate(alphas, axis=1) * acc_scr[hd, w]
                                  + _dot(vt, jnp.concatenate(ps, axis=1)))

    def step(t, buf, own=None):
        scores(t, buf)
        values(t - 1, 1 - buf, own)

    @pl.when(i == 0)
    def _():
        load_queries(q_ref)
        scores(0, 0)

    def pair(k, carry):
        t = 1 + 2 * k
        step(t, 1)
        step(t + 1, 0)
        return carry

    def four_pairs(k, carry):
        for n in range(4):
            carry = pair(4 * k + n, carry)
        return carry

    lax.fori_loop(0, i // 4, four_pairs, 0)
    lax.fori_loop(i // 4 * 4, i, pair, 0)
    step(last, 1, own=0)
    load_queries(qnext_ref)
    scores(0, 0)
    values(last, 1, 1)

    for hd in range(H_MLA):
        for w in range(tq // MXU_COLS):
            o = acc_scr[hd, w, 0:D_VA, :] * (1.0 / acc_scr[hd, w, D_VA:D_VA + 1, :])
            o_ref[w * MXU_COLS:(w + 1) * MXU_COLS, hd * D_VA:(hd + 1) * D_VA] = o.T


def _mla_attn_first(q, kn, kpe, vt, *, batch, t, tq, tk):
    assert t % tq == 0 and tq == 2 * tk and tk % CHUNK == 0
    nq = t // tq
    vcols = H_MLA * D_VA
    resident = lambda a: pl.BlockSpec((t, a.shape[1]), lambda b, i: (b, 0), pipeline_mode=pl.Buffered(1))
    scratch = [pltpu.VMEM((H_MLA, Q_HEAD_COLS, tq), BF16),
               pltpu.VMEM((H_MLA, 1, tq), F32),
               pltpu.VMEM((H_MLA, tq // MXU_COLS, VT_ROWS, MXU_COLS), F32),
               pltpu.VMEM((2, H_MLA, tq // LANES, tk, LANES), F32)]
    return pl.pallas_call(
        functools.partial(_mla_first_kernel, tq=tq, tk=tk),
        grid=(batch, nq),
        in_specs=[pl.BlockSpec((tq, q.shape[1]), lambda b, i: (b * nq + i, 0)),
                  pl.BlockSpec((tq, q.shape[1]), lambda b, i: (b * nq + jnp.minimum(i + 1, nq - 1), 0)),
                  resident(kn), resident(kpe),
                  pl.BlockSpec((None, H_MLA * VT_ROWS, t), lambda b, i: (b, 0, 0), pipeline_mode=pl.Buffered(1))],
        out_specs=pl.BlockSpec((tq, vcols), lambda b, i: (b * nq + i, 0)),
        out_shape=jax.ShapeDtypeStruct((batch * t, vcols), F32),
        scratch_shapes=scratch,
        compiler_params=_params("parallel", "arbitrary"),
        name="mla_attn_first",
    )(q, q, kn, kpe, vt)


def _sb_kernel(q_ref, *refs, tq, tsub, tk, past):
    if past:
        kp_ref, vp_ref, k_ref, v_ref, o_ref, c_scr, acc_scr = refs
    else:
        k_ref, v_ref, o_ref, c_scr, acc_scr = refs
    i = pl.program_id(1)
    nsub = tq // tsub
    c_scr[...] = jnp.zeros(c_scr.shape, F32)
    acc_scr[...] = jnp.zeros(acc_scr.shape, F32)
    heads = [slice(hd * D_SB, (hd + 1) * D_SB) for hd in range(H_SB)]
    chains = [(u, hd) for u in range(nsub) for hd in range(H_SB)]

    def later_eq(n):
        return (lax.broadcasted_iota(jnp.int32, (n, n), 1)
                >= lax.broadcasted_iota(jnp.int32, (n, n), 0)).astype(BF16)

    def new_rows(r0, n):
        return ([k_ref[pl.ds(r0, n), cols] for cols in heads], [v_ref[pl.ds(r0, n), cols] for cols in heads])

    def cached_rows(r0, n):
        return ([kp_ref[pl.ds(r0, n), hd, :].astype(BF16) for hd in range(H_SB)],
                [vp_ref[pl.ds(r0, n), hd, :].astype(BF16) for hd in range(H_SB)])

    def update(kvs, live, tri, before):
        zs = [_dot_nt(kvs[u][0][hd], q_ref[u * tsub:(u + 1) * tsub, heads[hd]]) for u, hd in chains]
        es, incls = [], []
        for z in zs:
            ls = jnp.minimum(-z, 0.0) - jnp.log2(1.0 + jnp.exp2(-jnp.abs(z)))
            lsm = ls if before is None else jnp.where(before, ls, 0.0)
            incl = _dot(tri, lsm.astype(BF16))
            incls.append(incl)
            es.append(z + incl if before is None else z + ls + (incl - lsm))
        top = None
        for n, (u, hd) in enumerate(chains):
            c = c_scr[u, hd]
            e = es[n] + c
            if live[u] is not None:
                e = jnp.where(live[u], e, MASK_VALUE)
            w = jnp.exp2(e)
            if before is not None:
                w = jnp.where(before, w, 0.0)
            acc_scr[u, hd] += _dot_tn(kvs[u][1][hd], w.astype(BF16))
            c_new = c + incls[n][0:1, :]
            c_scr[u, hd] = c_new
            top_c = jnp.max(c_new)
            if live[u] is not None:
                top_c = jnp.where(live[u], top_c, MASK_VALUE)
            top = top_c if top is None else jnp.maximum(top, top_c)
        return top

    own0 = i * tq
    before = (lax.broadcasted_iota(jnp.int32, (tsub, tsub), 0)
              < lax.broadcasted_iota(jnp.int32, (tsub, tsub), 1))
    top = update([new_rows(pl.multiple_of(own0 + u * tsub, tsub), tsub) for u in range(nsub)],
                 [None] * nsub, later_eq(tsub), before)
    tri_k = later_eq(tk)
    older = cached_rows if past else new_rows
    newest = [(past + own0 + u * tsub) // tk - 1 for u in range(nsub)]

    def cond(st):
        k, top = st
        return jnp.logical_and(newest[-1] - k >= 0, top > SB_EXP2_ZERO)

    def body(st):
        k, _ = st
        kvs, live = [], []
        for u in range(nsub):
            j = newest[u] - k
            kvs.append(older(pl.multiple_of(jnp.maximum(j, 0) * tk, tk), tk))
            live.append(None if u == nsub - 1 else j >= 0)
        return k + 1, update(kvs, live, tri_k, None)

    lax.while_loop(cond, body, (0, top))
    for u, hd in chains:
        o_ref[u * tsub:(u + 1) * tsub, heads[hd]] = acc_scr[u, hd].T


def _sb_attn(q, k, v, *, batch, t, tq, tk, past=0, k_cache=None, v_cache=None, layer=0):
    tsub = tq if past else tk
    assert t % tq == 0 and past % tk == 0 and tq % tsub == 0 and (past == 0 or t == tq)
    nq = t // tq
    nsub = tq // tsub
    cols = H_SB * D_SB
    q_spec = pl.BlockSpec((tq, cols), lambda b, i: (b * nq + i, 0))
    if past:
        cache = pl.BlockSpec((None, None, past, H_SB, D_SB), lambda b, i: (layer, b, 0, 0, 0))
        new = pl.BlockSpec((t, cols), lambda b, i: (b, 0))
        in_specs, operands = [q_spec, cache, cache, new, new], (q, k_cache, v_cache, k, v)
    else:
        resident = pl.BlockSpec((t, cols), lambda b, i: (b, 0), pipeline_mode=pl.Buffered(1))
        in_specs, operands = [q_spec, resident, resident], (q, k, v)
    return pl.pallas_call(
        functools.partial(_sb_kernel, tq=tq, tsub=tsub, tk=tk, past=past),
        grid=(batch, nq),
        in_specs=in_specs,
        out_specs=q_spec,
        out_shape=jax.ShapeDtypeStruct((batch * t, cols), F32),
        scratch_shapes=[pltpu.VMEM((nsub, H_SB, 1, tsub), F32), pltpu.VMEM((nsub, H_SB, D_SB, tsub), F32)],
        compiler_params=_params("parallel", "arbitrary"),
        name="sb_attn",
    )(*operands)


def _post_kernel(oa_ref, ob_ref, x_ref, mod_ref, goa_ref, gob_ref, gnf_ref, wout_ref, wg_ref, wu_ref, wd_ref,
                 wconv_ref, bconv_ref, cbuf_ref, y_ref, nconv_ref, x1_scr, h_scr, u_scr,
                 *, seg, nseg, d, per_batch):
    i = pl.program_id(0)
    na = oa_ref.shape[1]
    ma = _rms_rows(oa_ref[...], goa_ref[...]).astype(BF16)
    mb = _rms_rows(ob_ref[...], gob_ref[...]).astype(BF16)
    mix = _dot(ma, wout_ref[0:na, :]) + _dot(mb, wout_ref[na:, :])
    for s in range(nseg):
        rows = slice(s * seg, (s + 1) * seg)
        m = mod_ref[s]
        x1 = x_ref[rows, :] + m[:, 2 * d:3 * d] * mix[rows, :]
        x1_scr[rows, :] = x1
        h = _rms_rows(x1, gnf_ref[...]) * (1.0 + m[:, 4 * d:5 * d]) + m[:, 3 * d:4 * d]
        h_scr[rows, :] = h.astype(BF16)
    h = h_scr[...]

    if nseg == 1:
        first = i % per_batch == 0

        @pl.when(first)
        def _():
            u_scr[0, 8 - (CONV_W - 1):8, :] = cbuf_ref[0]

        @pl.when(jnp.logical_not(first))
        def _():
            u_scr[0, 0:8, :] = u_scr[0, seg:seg + 8, :]
    else:
        for s in range(nseg):
            u_scr[s, 8 - (CONV_W - 1):8, :] = cbuf_ref[s]

    ff = wg_ref.shape[1]
    bounds = [ff * c // FF_CHUNKS // MXU_COLS * MXU_COLS for c in range(FF_CHUNKS)] + [ff]
    chunks = [slice(bounds[c], bounds[c + 1]) for c in range(FF_CHUNKS)]
    us = [_dot(h, wg_ref[:, c]) for c in chunks]
    ups = [_dot(h, wu_ref[:, c]) for c in chunks]
    down = None
    for c, u, up in zip(chunks, us, ups):
        for s in range(nseg):
            u_scr[s, 8:8 + seg, c] = u[s * seg:(s + 1) * seg, :]
            nconv_ref[s, :, c] = u[(s + 1) * seg - (CONV_W - 1):(s + 1) * seg, :]
        parts = []
        for s in range(nseg):
            uc = bconv_ref[:, c]
            for tap in range(CONV_W):
                lo = 8 - (CONV_W - 1) + tap
                uc = uc + u_scr[s, lo:lo + seg, c] * wconv_ref[tap:tap + 1, c]
            parts.append(uc)
        uc = parts[0] if nseg == 1 else jnp.concatenate(parts, axis=0)
        act = (uc * jax.nn.sigmoid(uc) * up).astype(BF16)
        part = _dot(act, wd_ref[c, :])
        down = part if down is None else down + part
    for s in range(nseg):
        rows = slice(s * seg, (s + 1) * seg)
        y_ref[rows, :] = x1_scr[rows, :] + mod_ref[s][:, 5 * d:6 * d] * down[rows, :]


def _post(oa, ob, x2, mod3, goa, gob, gnf, wout, wg, wu, wd, wconv, bconv, cbuf, *, layer, batch, t, tm_max):
    r, d = x2.shape
    ff = wg.shape[-1]
    tm, seg, nseg = _row_tiling(batch, t, tm_max)
    per_batch = max(1, t // tm)
    full = lambda shape: pl.BlockSpec(shape, lambda i: (0,) * len(shape))
    rows = lambda w: pl.BlockSpec((tm, w), lambda i: (i, 0))
    if nseg == 1:
        state_spec = pl.BlockSpec((1, CONV_W - 1, ff), lambda i: (i // per_batch, 0, 0))
    else:
        state_spec = pl.BlockSpec((nseg, CONV_W - 1, ff), lambda i: (i, 0, 0))
    return pl.pallas_call(
        functools.partial(_post_kernel, seg=seg, nseg=nseg, d=d, per_batch=per_batch),
        grid=(r // tm,),
        in_specs=[rows(oa.shape[1]), rows(ob.shape[1]), rows(d), _mod_spec(t, tm, nseg, 6 * d),
                  full((1, oa.shape[1])), full((1, ob.shape[1])), full((1, d)),
                  _layer_spec(wout, layer), _layer_spec(wg, layer), _layer_spec(wu, layer),
                  _layer_spec(wd, layer),
                  full((CONV_W, ff)), full((1, ff)), state_spec],
        out_specs=[rows(d), state_spec],
        out_shape=(jax.ShapeDtypeStruct((r, d), F32),
                   jax.ShapeDtypeStruct((batch, CONV_W - 1, ff), F32)),
        scratch_shapes=[pltpu.VMEM((tm, d), F32), pltpu.VMEM((tm, d), BF16),
                        pltpu.VMEM((nseg, seg + 8, ff), F32)],
        compiler_params=_params("arbitrary"),
        name="post_ffn",
    )(oa, ob, x2, mod3, goa, gob, gnf, wout, wg, wu, wd, wconv, bconv, cbuf)


def _in_col_runs():
    qa = H_MLA * (D_NOPE + D_ROPE)
    half = D_ROPE // 2
    runs = []
    for h in range(H_MLA):
        base = h * (D_NOPE + D_ROPE)
        pe = base + D_NOPE
        runs += [(base, pe + D_ROPE), (pe + half, pe + D_ROPE), (pe, pe + half)]
    runs.append((qa, qa + D_LATENT))
    kp = qa + D_LATENT
    runs += [(kp, kp + D_ROPE), (kp + half, kp + D_ROPE), (kp, kp + half)]
    runs.append((kp + D_ROPE, kp + D_ROPE + 3 * H_SB * D_SB))
    return runs


def _win_kernel(w_ref, o_ref):
    w = w_ref[...]
    o_ref[...] = jnp.concatenate([w[:, a:b] for a, b in _in_col_runs()], axis=1).astype(o_ref.dtype)


def _permute_w_in(w_in):
    depth, d, n = w_in.shape
    x = w_in.reshape(depth * d, n)
    tm = 256
    out = pl.pallas_call(
        _win_kernel,
        grid=(depth * d // tm,),
        in_specs=[pl.BlockSpec((tm, n), lambda i: (i, 0))],
        out_specs=pl.BlockSpec((tm, C_END), lambda i: (i, 0)),
        out_shape=jax.ShapeDtypeStruct((depth * d, C_END), BF16),
        compiler_params=_params("parallel"),
        name="permute_w_in",
    )(x)
    return out.reshape(depth, d, C_END)


def _rope_gain(g):
    half = D_ROPE // 2
    return jnp.concatenate([g, g[..., half:], g[..., :half]], axis=-1)


def _rope_tables(pos0, t):
    half = D_ROPE // 2
    freqs = ROPE_THETA ** (-jnp.arange(half, dtype=F32) / half)

    def direct(pos):
        ang = pos.astype(F32)[:, None] * freqs[None, :]
        return jnp.cos(ang), jnp.sin(ang)

    if t % LANES == 0 and pos0 % LANES == 0 and t > LANES:
        ca, sa = direct(pos0 + LANES * jnp.arange(t // LANES, dtype=jnp.int32))
        cb, sb = direct(jnp.arange(LANES, dtype=jnp.int32))
        cos = (ca[:, None] * cb[None] - sa[:, None] * sb[None]).reshape(t, half)
        sin = (sa[:, None] * cb[None] + ca[:, None] * sb[None]).reshape(t, half)
    else:
        cos, sin = direct(pos0 + jnp.arange(t, dtype=jnp.int32))
    return jnp.tile(cos, (1, 4)), jnp.concatenate([-sin, sin, sin, -sin], axis=-1)


def kernel(x_prompt, x_sample, c_prompt, c_sample, cache_mla_latent, cache_mla_krope, cache_sb_k, cache_sb_v, state_ffn_conv, w_ada, b_ada, g_norm_mix, g_norm_ffn, w_in, g_kv_latent, g_q_nope, g_q_rope, g_k_nope, g_k_rope, w_uk, w_uv, g_out_mla, g_out_sb, w_out, w_gate, w_up, w_conv, b_conv, w_down):
    depth = w_in.shape[0]
    bp, tp, d = x_prompt.shape
    bs, ts, _ = x_sample.shape
    past = cache_mla_latent.shape[2]
    ff = w_gate.shape[2]
    sb = H_SB * D_SB

    nb = bp + bs
    nb_pad = -(-nb // 8) * 8
    c_all = jnp.concatenate([c_prompt, c_sample, jnp.zeros((nb_pad - nb, d), F32)], axis=0)
    mod = _ada_all(c_all, w_ada, b_ada)

    w_in_p = _permute_w_in(w_in)
    w_uk16, w_uv16, w_out16 = _to_bf16(w_uk), _to_bf16(w_uv), _to_bf16(w_out)
    w_uvt16 = jnp.swapaxes(w_uv16, 1, 2)
    w_gate16, w_up16, w_down16 = _to_bf16(w_gate), _to_bf16(w_up), _to_bf16(w_down)
    gqr, gkr = _rope_gain(g_q_rope), _rope_gain(g_k_rope)

    cos_p, sin_p = _rope_tables(0, tp)
    cos_p, sin_p = jnp.tile(cos_p, (bp, 1)), jnp.tile(sin_p, (bp, 1))
    cos_s, sin_s = _rope_tables(past, ts)
    cos_s, sin_s = jnp.tile(cos_s, (bs, 1)), jnp.tile(sin_s, (bs, 1))

    lat_cache_rows = cache_mla_latent.reshape(-1, D_LATENT)

    def layer(l, x2, mod3, cos, sin, batch, t, p, stacks):
        row = lambda a: a[l][None, :]
        qm, lat_st, kpe_st, kpe128, qb, kb_st, vb_st, kb16, vb16 = _inproj(
            x2, mod3, row(g_norm_mix), w_in_p, row(g_q_nope), row(gqr), row(g_kv_latent), row(gkr),
            cos, sin, batch=batch, t=t, tm_max=512, layer=l, depth=depth, stacks=stacks)
        new_lat = dict(row0=l * batch * t, n_rows=batch * t)
        lat_rows = lat_st.reshape(-1, D_LATENT)
        if p:
            cbuf = state_ffn_conv[l]
            knp, vp = _kvup(lat_cache_rows, w_uk16, w_uv16, row(g_k_nope), layer=l, tm_max=2048,
                            row0=l * batch * p, n_rows=batch * p)
            knn, vn = _kvup(lat_rows, w_uk16, w_uv16, row(g_k_nope), layer=l, tm_max=512, **new_lat)
            oa = _mla_attn_cached(qm, knp, cache_mla_krope, vp, knn, kpe128, vn,
                                  layer=l, batch=batch, t=t, past=p, tk=256)
            ob = _sb_attn(qb, kb16, vb16, batch=batch, t=t, tq=t, tk=256, past=p,
                          k_cache=cache_sb_k, v_cache=cache_sb_v, layer=l)
        else:
            cbuf = jnp.zeros((batch, CONV_W - 1, ff), F32)
            kn, vt = _kvup(lat_rows, w_uk16, w_uvt16, row(g_k_nope), layer=l, tm_max=2048, per_batch_t=t,
                           **new_lat)
            oa = _mla_attn_first(qm, kn, kpe128, vt, batch=batch, t=t, tq=512, tk=256)
            ob = _sb_attn(qb, kb16, vb16, batch=batch, t=t, tq=512, tk=256)
        y, nconv = _post(oa, ob, x2, mod3, row(g_out_mla), row(g_out_sb), row(g_norm_ffn), w_out16,
                         w_gate16, w_up16, w_down16, w_conv[l], row(b_conv), cbuf,
                         layer=l, batch=batch, t=t, tm_max=512)
        return y, (lat_st, kpe_st, kb_st, vb_st), nconv

    xp = x_prompt.reshape(bp * tp, d)
    xs = x_sample.reshape(bs * ts, d)
    st_p = st_s = None
    conv_p, conv_s = [], []
    for l in range(depth):
        mod_p = mod[l, :bp].reshape(bp, 1, 6 * d)
        mod_s = mod[l, bp:nb].reshape(bs, 1, 6 * d)
        xp, st_p, nc_p = layer(l, xp, mod_p, cos_p, sin_p, bp, tp, 0, st_p)
        xs, st_s, nc_s = layer(l, xs, mod_s, cos_s, sin_s, bs, ts, past, st_s)
        conv_p.append(nc_p)
        conv_s.append(nc_s)

    def step_outputs(stacks, convs, batch, t):
        lat_st, kpe_st, kb_st, vb_st = stacks
        return (lat_st.reshape(depth, batch, t, D_LATENT), kpe_st.reshape(depth, batch, t, D_ROPE),
                kb_st.reshape(depth, batch, t, H_SB, D_SB), vb_st.reshape(depth, batch, t, H_SB, D_SB),
                jnp.stack(convs))

    return ((xp.reshape(bp, tp, d), xs.reshape(bs, ts, d))
            + step_outputs(st_p, conv_p, bp, tp) + step_outputs(st_s, conv_s, bs, ts))
```
